```python
import math
import jax, jax.numpy as jnp
from jax import lax
import numpy as np

D_MODEL = 1024
BATCH = 4
SEQ = 4096
DEPTH = 2

EPS = 1e-6
S5_WIDTH = D_MODEL // 2
S5_GROUP = 16
S5_GROUPS = S5_WIDTH // S5_GROUP
S5_STATE = 64
DT_MIN = 1e-3
DT_MAX = 1e-1
RET_HEADS = 4
RET_DK = D_MODEL // 16
RET_DV = 2 * RET_DK
RET_QK_WIDTH = RET_HEADS * RET_DK
RET_V_WIDTH = RET_HEADS * RET_DV
RET_CHUNK = 128
ROPE_BASE = 10000.0
MAX_POS_OFFSET = 1024
LRU_WIDTH = D_MODEL // 2
LRU_BLOCKS = 8
LRU_BLOCK = LRU_WIDTH // LRU_BLOCKS
CONV_WIDTH = 4
LRU_C = 8.0
N_BRANCH = 3
BRANCH_WIDTH = 512
IN_SIZES = (S5_WIDTH, RET_QK_WIDTH, RET_QK_WIDTH, RET_V_WIDTH, RET_V_WIDTH, LRU_WIDTH, LRU_WIDTH, N_BRANCH * D_MODEL)
IN_WIDTH = S5_WIDTH + 2 * RET_QK_WIDTH + 2 * RET_V_WIDTH + 2 * LRU_WIDTH + N_BRANCH * D_MODEL
PEER_HEADS = 8
PEER_NKEYS = 128
PEER_EXPERTS = PEER_NKEYS * PEER_NKEYS
PEER_TOPK = 16
PEER_DKEY = 128
PEER_HALF = PEER_DKEY // 2
PEER_TOKEN_BLOCK = 128
PEER_V_SCALE = 0.25

kernel_name = "hybrid_s5_retnet_rglru_peer"


def rmsnorm(x, w):
    xf = x.astype(jnp.float32)
    y = xf * lax.rsqrt(jnp.mean(xf * xf, axis=-1, keepdims=True) + EPS)
    return (y * w.astype(jnp.float32)).astype(x.dtype)


def linear_combine(e1, e2):
    a1, b1 = e1
    a2, b2 = e2
    return a1 * a2, a2 * b1 + b2


def s5_mixer(u, lam_re, lam_im, log_dt, b_re, b_im, c_re, c_im, d_skip, w_glu, b_glu):
    bsz, s, _ = u.shape
    f32 = jnp.float32
    uf = u.astype(f32).reshape(bsz, s, S5_GROUPS, S5_GROUP)
    lam = lax.complex(lam_re.astype(f32), lam_im.astype(f32))
    dt = jnp.exp(log_dt.astype(f32))[:, None]
    lam_bar = jnp.exp(lam * dt)
    b = lax.complex(b_re.astype(f32), b_im.astype(f32))
    b_bar = ((lam_bar - 1.0) / lam)[..., None] * b
    c = lax.complex(c_re.astype(f32), c_im.astype(f32))
    bu = jnp.einsum('gph,bsgh->bsgp', b_bar, uf)
    a = jnp.broadcast_to(lam_bar, (1, s) + lam_bar.shape)
    _, h = lax.associative_scan(linear_combine, (a, bu), axis=1)
    y = jnp.real(jnp.einsum('ghp,bsgp->bsgh', c, h)) + d_skip.astype(f32).reshape(S5_GROUPS, S5_GROUP) * uf
    z = jax.nn.gelu(y.reshape(bsz, s, S5_WIDTH))
    out = z * jax.nn.sigmoid(z @ w_glu.astype(f32) + b_glu.astype(f32))
    return out.astype(u.dtype)


def rotary(t, positions):
    half = t.shape[-1] // 2
    inv_freq = ROPE_BASE ** (-jnp.arange(half, dtype=jnp.float32) / half)
    ang = positions.astype(jnp.float32)[..., None] * inv_freq
    cos = jnp.cos(ang)[:, :, None, :]
    sin = jnp.sin(ang)[:, :, None, :]
    tf = t.astype(jnp.float32)
    t1, t2 = tf[..., :half], tf[..., half:]
    return jnp.concatenate([t1 * cos - t2 * sin, t1 * sin + t2 * cos], axis=-1)


def retention_mixer(q, k, v, g, positions, gn_w):
    bsz, s, _ = q.shape
    nc = s // RET_CHUNK
    f32 = jnp.float32
    q = rotary(q.reshape(bsz, s, RET_HEADS, RET_DK), positions)
    k = rotary(k.reshape(bsz, s, RET_HEADS, RET_DK), positions) * (RET_DK ** -0.5)
    v = v.astype(f32).reshape(bsz, s, RET_HEADS, RET_DV)

    def chunks(t):
        return t.reshape(bsz, nc, RET_CHUNK, RET_HEADS, -1).transpose(0, 3, 1, 2, 4)

    qc, kc, vc = chunks(q), chunks(k), chunks(v)
    log_gamma = jnp.log1p(-(2.0 ** (-5.0 - jnp.arange(RET_HEADS, dtype=f32))))
    idx = jnp.arange(RET_CHUNK, dtype=f32)
    diff = idx[:, None] - idx[None, :]
    decay = jnp.where(diff >= 0, jnp.exp(log_gamma[:, None, None] * jnp.maximum(diff, 0.0)), 0.0)
    scores = jnp.einsum('bhcid,bhcjd->bhcij', qc, kc) * decay[:, None]
    intra = jnp.einsum('bhcij,bhcjv->bhciv', scores, vc)
    zeta = jnp.exp(log_gamma[:, None] * (RET_CHUNK - 1 - idx))
    kv = jnp.einsum('bhcjd,hj,bhcjv->bhcdv', kc, zeta, vc)
    chunk_decay = jnp.exp(log_gamma * RET_CHUNK)[None, :, None, None]

    def step(state, kv_c):
        return chunk_decay * state + kv_c, state

    init = jnp.zeros((bsz, RET_HEADS, RET_DK, RET_DV), f32)
    _, prev = lax.scan(step, init, jnp.moveaxis(kv, 2, 0))
    prev = jnp.moveaxis(prev, 0, 2)
    xi = jnp.exp(log_gamma[:, None] * (idx + 1.0))
    cross = jnp.einsum('bhcid,bhcdv->bhciv', qc, prev) * xi[:, None, :, None]
    o = (intra + cross).transpose(0, 2, 3, 1, 4).reshape(bsz, s, RET_HEADS, RET_DV)
    mu = jnp.mean(o, axis=-1, keepdims=True)
    var = jnp.mean(jnp.square(o - mu), axis=-1, keepdims=True)
    o = ((o - mu) * lax.rsqrt(var + EPS)).reshape(bsz, s, RET_V_WIDTH) * gn_w.astype(f32)
    return (jax.nn.silu(g.astype(f32)) * o).astype(g.dtype)


def rglru_mixer(xb, gate_in, conv_w, conv_b, w_r, b_r, w_i, b_i, lam):
    bsz, s, w = xb.shape
    f32 = jnp.float32
    xc = lax.conv_general_dilated(xb, conv_w[:, None, :], window_strides=(1,), padding=[(CONV_WIDTH - 1, 0)],
                                  dimension_numbers=('NWC', 'WIO', 'NWC'), feature_group_count=w) + conv_b
    xblk = xc.reshape(bsz, s, LRU_BLOCKS, LRU_BLOCK)
    r = jax.nn.sigmoid(jnp.einsum('bsnc,ncd->bsnd', xblk, w_r).reshape(bsz, s, w) + b_r)
    i = jax.nn.sigmoid(jnp.einsum('bsnc,ncd->bsnd', xblk, w_i).reshape(bsz, s, w) + b_i)
    log_a = -LRU_C * r.astype(f32) * jax.nn.softplus(-lam.astype(f32))
    a = jnp.exp(log_a)
    mult = jnp.sqrt(-jnp.expm1(2.0 * log_a))
    bx = mult * (i * xc).astype(f32)
    _, h = lax.associative_scan(linear_combine, (a, bx), axis=1)
    return (h * jax.nn.gelu(gate_in.astype(f32))).astype(xb.dtype)


def hybrid_mixer(xn, positions, w_in, s5_lam_re, s5_lam_im, s5_log_dt, s5_b_re, s5_b_im, s5_c_re, s5_c_im,
                 s5_d, s5_w_glu, s5_b_glu, ret_gn_w, lru_conv_w, lru_conv_b, lru_w_r, lru_b_r, lru_w_i, lru_b_i,
                 lru_lam, w_branch, w_out):
    bsz, s, _ = xn.shape
    proj = xn @ w_in
    splits = np.cumsum(IN_SIZES)[:-1].tolist()
    u_s5, q, k, v, g_ret, x_lru, g_lru, gate_logits = jnp.split(proj, splits, axis=-1)
    y_s5 = s5_mixer(u_s5, s5_lam_re, s5_lam_im, s5_log_dt, s5_b_re, s5_b_im, s5_c_re, s5_c_im, s5_d, s5_w_glu, s5_b_glu)
    y_ret = retention_mixer(q, k, v, g_ret, positions, ret_gn_w)
    y_lru = rglru_mixer(x_lru, g_lru, lru_conv_w, lru_conv_b, lru_w_r, lru_b_r, lru_w_i, lru_b_i, lru_lam)
    ys = jnp.stack([y_s5, y_ret, y_lru], axis=2)
    branch = jnp.einsum('bsnw,nwd->bsnd', ys, w_branch)
    gates = jax.nn.sigmoid(gate_logits.reshape(bsz, s, N_BRANCH, D_MODEL))
    merged = jnp.sum(gates * branch, axis=2)
    return merged @ w_out


def peer_ffn(xn, w_query, sub_keys, expert_u, expert_v):
    bsz, s, d = xn.shape
    t = bsz * s
    xt = xn.reshape(t, d)
    q = (xt @ w_query).astype(jnp.float32).reshape(t, PEER_HEADS, 2, PEER_HALF)
    sc = jnp.einsum('thpd,hpkd->thpk', q, sub_keys.astype(jnp.float32))
    s1, i1 = lax.top_k(sc[:, :, 0], PEER_TOPK)
    s2, i2 = lax.top_k(sc[:, :, 1], PEER_TOPK)
    cand = (s1[..., :, None] + s2[..., None, :]).reshape(t, PEER_HEADS, PEER_TOPK * PEER_TOPK)
    cand_idx = (i1[..., :, None] * PEER_NKEYS + i2[..., None, :]).reshape(t, PEER_HEADS, PEER_TOPK * PEER_TOPK)
    top_s, pos = lax.top_k(cand, PEER_TOPK)
    experts = jnp.take_along_axis(cand_idx, pos, axis=-1)
    gates = jax.nn.softmax(top_s, axis=-1).astype(xn.dtype)
    nb = t // PEER_TOKEN_BLOCK

    def block(args):
        xb, eb, gb = args
        act = jax.nn.gelu(jnp.einsum('lhkd,ld->lhk', expert_u[eb], xb))
        return jnp.einsum('lhk,lhkd->ld', act * gb, expert_v[eb])

    out = lax.map(block, (xt.reshape(nb, PEER_TOKEN_BLOCK, d),
                          experts.reshape(nb, PEER_TOKEN_BLOCK, PEER_HEADS, PEER_TOPK),
                          gates.reshape(nb, PEER_TOKEN_BLOCK, PEER_HEADS, PEER_TOPK)))
    return out.reshape(bsz, s, d)


def setup_inputs(seed: int = 0) -> dict:
    key = jax.random.key(seed)
    ks = jax.random.split(key, 32)
    f32 = jnp.float32
    L, D = DEPTH, D_MODEL
    G, P, H16 = S5_GROUPS, S5_STATE, S5_GROUP

    def nrm(k, shape, scale):
        return jax.random.normal(k, shape, f32) * scale

    x = nrm(ks[0], (BATCH, SEQ, D), 1.0)
    positions = (jax.random.randint(ks[1], (BATCH, 1), 0, MAX_POS_OFFSET, dtype=jnp.int32)
                 + jnp.arange(SEQ, dtype=jnp.int32)[None, :])
    norm_mix = 1.0 + nrm(ks[2], (L, D), 0.02)
    w_in = nrm(ks[3], (L, D, IN_WIDTH), D ** -0.5)
    n = jnp.arange(P, dtype=f32)
    s5_lam_re = -0.5 + nrm(ks[4], (L, G, P), 0.01)
    s5_lam_im = math.pi * n + nrm(ks[5], (L, G, P), 0.01)
    s5_log_dt = jax.random.uniform(ks[6], (L, G), f32, math.log(DT_MIN), math.log(DT_MAX))
    s5_b_re = nrm(ks[7], (L, G, P, H16), (2.0 * H16) ** -0.5)
    s5_b_im = nrm(ks[8], (L, G, P, H16), (2.0 * H16) ** -0.5)
    s5_c_re = nrm(ks[9], (L, G, H16, P), 0.5)
    s5_c_im = nrm(ks[10], (L, G, H16, P), 0.5)
    s5_d = nrm(ks[11], (L, S5_WIDTH), 1.0)
    s5_w_glu = nrm(ks[12], (L, S5_WIDTH, S5_WIDTH), S5_WIDTH ** -0.5)
    s5_b_glu = nrm(ks[13], (L, S5_WIDTH), 0.01)
    ret_gn_w = 1.0 + nrm(ks[14], (L, RET_V_WIDTH), 0.02)
    lru_conv_w = nrm(ks[15], (L, CONV_WIDTH, LRU_WIDTH), CONV_WIDTH ** -0.5)
    lru_conv_b = nrm(ks[16], (L, LRU_WIDTH), 0.01)
    lru_w_r = nrm(ks[17], (L, LRU_BLOCKS, LRU_BLOCK, LRU_BLOCK), LRU_BLOCK ** -0.5)
    lru_b_r = nrm(ks[18], (L, LRU_WIDTH), 0.01)
    lru_w_i = nrm(ks[19], (L, LRU_BLOCKS, LRU_BLOCK, LRU_BLOCK), LRU_BLOCK ** -0.5)
    lru_b_i = nrm(ks[20], (L, LRU_WIDTH), 0.01)
    a0 = jax.random.uniform(ks[21], (L, LRU_WIDTH), f32, 0.9, 0.999)
    a_base = a0 ** (1.0 / LRU_C)
    lru_lam = jnp.log(a_base) - jnp.log1p(-a_base)
    w_branch = nrm(ks[22], (L, N_BRANCH, BRANCH_WIDTH, D), BRANCH_WIDTH ** -0.5)
    w_out = nrm(ks[23], (L, D, D), D ** -0.5)
    norm_ffn = 1.0 + nrm(ks[24], (L, D), 0.02)
    peer_w_query = nrm(ks[25], (L, D, PEER_HEADS * PEER_DKEY), D ** -0.5)
    peer_sub_keys = nrm(ks[26], (L, PEER_HEADS, 2, PEER_NKEYS, PEER_HALF), PEER_HALF ** -0.5)
    peer_u = nrm(ks[27], (L, PEER_EXPERTS, D), D ** -0.5)
    peer_v = nrm(ks[28], (L, PEER_EXPERTS, D), PEER_V_SCALE)
    final_norm = 1.0 + nrm(ks[29], (D,), 0.02)
    return {"x": x, "positions": positions, "norm_mix": norm_mix, "w_in": w_in,
            "s5_lam_re": s5_lam_re, "s5_lam_im": s5_lam_im, "s5_log_dt": s5_log_dt,
            "s5_b_re": s5_b_re, "s5_b_im": s5_b_im, "s5_c_re": s5_c_re, "s5_c_im": s5_c_im,
            "s5_d": s5_d, "s5_w_glu": s5_w_glu, "s5_b_glu": s5_b_glu, "ret_gn_w": ret_gn_w,
            "lru_conv_w": lru_conv_w, "lru_conv_b": lru_conv_b, "lru_w_r": lru_w_r, "lru_b_r": lru_b_r,
            "lru_w_i": lru_w_i, "lru_b_i": lru_b_i, "lru_lam": lru_lam, "w_branch": w_branch,
            "w_out": w_out, "norm_ffn": norm_ffn, "peer_w_query": peer_w_query,
            "peer_sub_keys": peer_sub_keys, "peer_u": peer_u, "peer_v": peer_v, "final_norm": final_norm}


def reference(x, positions, norm_mix, w_in, s5_lam_re, s5_lam_im, s5_log_dt, s5_b_re, s5_b_im, s5_c_re, s5_c_im,
              s5_d, s5_w_glu, s5_b_glu, ret_gn_w, lru_conv_w, lru_conv_b, lru_w_r, lru_b_r, lru_w_i, lru_b_i,
              lru_lam, w_branch, w_out, norm_ffn, peer_w_query, peer_sub_keys, peer_u, peer_v, final_norm):
    for l in range(DEPTH):
        xn = rmsnorm(x, norm_mix[l])
        x = x + hybrid_mixer(xn, positions, w_in[l], s5_lam_re[l], s5_lam_im[l], s5_log_dt[l], s5_b_re[l],
                             s5_b_im[l], s5_c_re[l], s5_c_im[l], s5_d[l], s5_w_glu[l], s5_b_glu[l], ret_gn_w[l],
                             lru_conv_w[l], lru_conv_b[l], lru_w_r[l], lru_b_r[l], lru_w_i[l], lru_b_i[l],
                             lru_lam[l], w_branch[l], w_out[l])
        xn = rmsnorm(x, norm_ffn[l])
        x = x + peer_ffn(xn, peer_w_query[l], peer_sub_keys[l], peer_u[l], peer_v[l])
    return rmsnorm(x, final_norm)
```

```python
import functools
import math

import jax
import jax.numpy as jnp
import numpy as np
from jax import lax
from jax.experimental import pallas as pl
from jax.experimental.pallas import tpu as pltpu

F32 = jnp.float32
BF16 = jnp.bfloat16

EPS = 1e-6
S5_GROUP = 16
S5_STATE = 64
RET_HEADS = 4
RET_CHUNK = 128
ROPE_BASE = 10000.0
LRU_BLOCKS = 8
CONV_WIDTH = 4
LRU_C = 8.0
N_BRANCH = 3
PEER_HEADS = 8
PEER_NKEYS = 128
PEER_TOPK = 16
PEER_HALF = 64

SUBLANES = 8
LANES = 128
VMEM_LIMIT = 56 * 1024 * 1024

_SQRT_2_OVER_PI = math.sqrt(2.0 / math.pi)


def _gelu(x):
    return 0.5 * x * (1.0 + jnp.tanh(_SQRT_2_OVER_PI * (x + 0.044715 * (x * x * x))))


def _sigmoid(x):
    return 1.0 / (1.0 + jnp.exp(-x))


def _rms(x, w):
    return x * lax.rsqrt(jnp.mean(x * x, axis=-1, keepdims=True) + EPS) * w


def _params(*sem):
    return pltpu.CompilerParams(dimension_semantics=sem, vmem_limit_bytes=VMEM_LIMIT)


def _norm_proj_kernel(x_ref, nw_ref, w_ref, o_ref, xn_ref):
    @pl.when(pl.program_id(1) == 0)
    def _():
        xn_ref[...] = _rms(x_ref[...], nw_ref[...]).astype(BF16)

    o_ref[...] = jnp.dot(xn_ref[...], w_ref[...], preferred_element_type=F32)


def _norm_proj(x2, norm_w, w_bf16, tm=1024, tn=1024):
    t, d = x2.shape
    n = w_bf16.shape[1]
    tm = min(tm, t)
    return pl.pallas_call(
        _norm_proj_kernel,
        grid=(t // tm, n // tn),
        in_specs=[
            pl.BlockSpec((tm, d), lambda i, j: (i, 0)),
            pl.BlockSpec((1, d), lambda i, j: (0, 0)),
            pl.BlockSpec((d, tn), lambda i, j: (0, j)),
        ],
        out_specs=pl.BlockSpec((tm, tn), lambda i, j: (i, j)),
        out_shape=jax.ShapeDtypeStruct((t, n), F32),
        scratch_shapes=[pltpu.VMEM((tm, d), BF16)],
        compiler_params=_params("parallel", "arbitrary"),
        name="norm_proj",
    )(x2, norm_w.reshape(1, d), w_bf16)


S5_LANE_CHUNK = 512


def _s5_kernel(u_ref, bmat_ref, cmat_ref, sc_ref, d_ref, wglu_ref, bglu_ref, o_ref,
               bu_ref, carry_ref, *, ns):
    @pl.when(pl.program_id(1) == 0)
    def _():
        carry_ref[...] = jnp.zeros_like(carry_ref)

    u = u_ref[...]
    lc = u.shape[0]
    bu_ref[...] = jnp.dot(u.astype(BF16), bmat_ref[...], preferred_element_type=F32)

    for c0 in range(0, ns, S5_LANE_CHUNK):
        re_sl = pl.ds(c0, S5_LANE_CHUNK)
        im_sl = pl.ds(ns + c0, S5_LANE_CHUNK)
        consts = [sc_ref[k, :, c0:c0 + S5_LANE_CHUNK] for k in range(8)]

        def body(r, carry, re_sl=re_sl, im_sl=im_sl, consts=consts):
            hpr, hpi = carry
            rows = pl.ds(pl.multiple_of(r * SUBLANES, SUBLANES), SUBLANES)
            br = bu_ref[rows, re_sl]
            bi = bu_ref[rows, im_sl]
            for lvl, dist in enumerate((1, 2, 4)):
                ar, ai = consts[2 * lvl], consts[2 * lvl + 1]
                sr = pltpu.roll(br, dist, 0)
                si = pltpu.roll(bi, dist, 0)
                br, bi = br + ar * sr - ai * si, bi + ar * si + ai * sr
            pr, pim = consts[6], consts[7]
            hr = br + pr * hpr - pim * hpi
            hi = bi + pr * hpi + pim * hpr
            bu_ref[rows, re_sl] = hr
            bu_ref[rows, im_sl] = hi
            return hr[SUBLANES - 1:SUBLANES, :], hi[SUBLANES - 1:SUBLANES, :]

        hr_last, hi_last = lax.fori_loop(
            0, lc // SUBLANES, body, (carry_ref[:, re_sl], carry_ref[:, im_sl]))
        carry_ref[:, re_sl] = hr_last
        carry_ref[:, im_sl] = hi_last

    y = jnp.dot(bu_ref[...].astype(BF16), cmat_ref[...], preferred_element_type=F32)
    z = _gelu(y + d_ref[...] * u)
    gl = jnp.dot(z.astype(BF16), wglu_ref[...], preferred_element_type=F32) + bglu_ref[...]
    o_ref[...] = z * _sigmoid(gl)


def _s5_consts(lam_re, lam_im, log_dt, b_re, b_im, c_re, c_im):
    g, p = lam_re.shape
    h16 = b_re.shape[-1]
    lam = lax.complex(lam_re.astype(F32), lam_im.astype(F32))
    dt = jnp.exp(log_dt.astype(F32))[:, None]
    lam_dt = lam * dt
    lam_bar = jnp.exp(lam_dt)
    b_bar = ((lam_bar - 1.0) / lam)[..., None] * lax.complex(b_re.astype(F32), b_im.astype(F32))
    eye = jnp.eye(g, dtype=F32)

    def bdiag_in(m):
        return jnp.einsum('gph,gk->ghkp', m, eye).reshape(g * h16, g * p)

    def bdiag_out(m):
        return jnp.einsum('ghp,gk->gpkh', m, eye).reshape(g * p, g * h16)

    bmat = jnp.concatenate([bdiag_in(jnp.real(b_bar)), bdiag_in(jnp.imag(b_bar))], axis=1)
    cmat = jnp.concatenate([bdiag_out(c_re.astype(F32)), bdiag_out(-c_im.astype(F32))], axis=0)

    row = jnp.arange(SUBLANES, dtype=F32)[:, None]
    planes = []
    for dist in (1, 2, 4):
        pw = jnp.exp(lam_dt * float(dist)).reshape(1, g * p)
        keep = row >= dist
        planes += [jnp.where(keep, jnp.real(pw), 0.0), jnp.where(keep, jnp.imag(pw), 0.0)]
    pw = jnp.exp(lam_dt.reshape(1, g * p) * (row + 1.0))
    planes += [jnp.real(pw), jnp.imag(pw)]
    return bmat.astype(BF16), cmat.astype(BF16), jnp.stack(planes).astype(F32)


def _s5_mixer(proj, bsz, seq, consts, d_skip, w_glu, b_glu, lc=256):
    bmat, cmat, sc = consts
    w = bmat.shape[0]
    ns = bmat.shape[1] // 2
    lc = min(lc, seq)
    nc = seq // lc
    const2 = lambda b, c: (0, 0)
    return pl.pallas_call(
        functools.partial(_s5_kernel, ns=ns),
        grid=(bsz, nc),
        in_specs=[
            pl.BlockSpec((lc, w), lambda b, c: (b * nc + c, 0)),
            pl.BlockSpec(bmat.shape, const2),
            pl.BlockSpec(cmat.shape, const2),
            pl.BlockSpec(sc.shape, lambda b, c: (0, 0, 0)),
            pl.BlockSpec((1, w), const2),
            pl.BlockSpec((w, w), const2),
            pl.BlockSpec((1, w), const2),
        ],
        out_specs=pl.BlockSpec((lc, w), lambda b, c: (b * nc + c, 0)),
        out_shape=jax.ShapeDtypeStruct((bsz * seq, w), F32),
        scratch_shapes=[pltpu.VMEM((lc, 2 * ns), F32), pltpu.VMEM((1, 2 * ns), F32)],
        compiler_params=_params("arbitrary", "arbitrary"),
        name="s5_mixer",
    )(proj, bmat, cmat, sc, d_skip.reshape(1, w), w_glu.astype(BF16), b_glu.reshape(1, w))


def _rope_kernel(pos_ref, freq_ref, cos_ref, sin_ref):
    ang = pos_ref[...].astype(F32) * freq_ref[...]
    cos_ref[...] = jnp.cos(ang)
    sin_ref[...] = jnp.sin(ang)


def _rope_tables(positions, half, tm=2048):
    t = positions.size
    tm = min(tm, t)
    reps = LANES // half
    inv_freq = ROPE_BASE ** (-jnp.arange(half, dtype=F32) / half)
    freq = jnp.tile(inv_freq, reps).reshape(1, LANES)
    return pl.pallas_call(
        _rope_kernel,
        grid=(t // tm,),
        in_specs=[pl.BlockSpec((tm, 1), lambda i: (i, 0)), pl.BlockSpec((1, LANES), lambda i: (0, 0))],
        out_specs=[pl.BlockSpec((tm, LANES), lambda i: (i, 0))] * 2,
        out_shape=[jax.ShapeDtypeStruct((t, LANES), F32)] * 2,
        compiler_params=_params("parallel"),
        name="rope_tables",
    )(positions.reshape(t, 1), freq)


def _ret_kernel(q_ref, k_ref, v_ref, g_ref, cos_ref, sin_ref, dec_ref, zeta_ref, xi_ref, cd_ref,
                gnw_ref, o_ref, state_ref, *, dk, dv):
    @pl.when(pl.program_id(1) == 0)
    def _():
        state_ref[...] = jnp.zeros_like(state_ref)

    lt = q_ref.shape[0]
    half = dk // 2
    qkw = RET_HEADS * dk
    cos = jnp.concatenate([cos_ref[...]] * (qkw // LANES), axis=1)
    sin = jnp.concatenate([sin_ref[...]] * (qkw // LANES), axis=1)
    lane = lax.broadcasted_iota(jnp.int32, (lt, qkw), 1)
    first = (lane % dk) < half
    sin_signed = jnp.where(first, -sin, sin)

    def rot(x):
        swapped = jnp.where(first, pltpu.roll(x, qkw - half, 1), pltpu.roll(x, half, 1))
        return x * cos + swapped * sin_signed

    q = rot(q_ref[...])
    k = rot(k_ref[...]) * (dk ** -0.5)
    v = v_ref[...]
    g = g_ref[...]
    gnw = gnw_ref[...]

    for c0 in range(0, lt, RET_CHUNK):
        for h in range(RET_HEADS):
            qh = q[c0:c0 + RET_CHUNK, h * dk:(h + 1) * dk]
            kh = k[c0:c0 + RET_CHUNK, h * dk:(h + 1) * dk]
            vh = v[c0:c0 + RET_CHUNK, h * dv:(h + 1) * dv]
            st = state_ref[h]
            scores = lax.dot_general(qh, kh, (((1,), (1,)), ((), ())),
                                     preferred_element_type=F32) * dec_ref[h]
            o = jnp.dot(scores, vh, preferred_element_type=F32)
            o = o + jnp.dot(qh, st, preferred_element_type=F32) * xi_ref[h]
            kz = kh * zeta_ref[h]
            kv = lax.dot_general(kz, vh, (((0,), (0,)), ((), ())), preferred_element_type=F32)
            state_ref[h] = cd_ref[h] * st + kv
            mu = jnp.mean(o, axis=-1, keepdims=True)
            oc = o - mu
            var = jnp.mean(oc * oc, axis=-1, keepdims=True)
            on = oc * lax.rsqrt(var + EPS) * gnw[:, h * dv:(h + 1) * dv]
            gh = g[c0:c0 + RET_CHUNK, h * dv:(h + 1) * dv]
            o_ref[c0:c0 + RET_CHUNK, h * dv:(h + 1) * dv] = gh * _sigmoid(gh) * on


def _ret_consts(dk, dv):
    log_gamma = jnp.log1p(-(2.0 ** (-5.0 - jnp.arange(RET_HEADS, dtype=F32))))
    idx = jnp.arange(RET_CHUNK, dtype=F32)
    diff = idx[:, None] - idx[None, :]
    decay = jnp.where(diff >= 0, jnp.exp(log_gamma[:, None, None] * jnp.maximum(diff, 0.0)), 0.0)
    zeta = jnp.exp(log_gamma[:, None] * (RET_CHUNK - 1 - idx))[:, :, None]
    xi = jnp.exp(log_gamma[:, None] * (idx + 1.0))[:, :, None]
    cd = jnp.broadcast_to(jnp.exp(log_gamma * RET_CHUNK)[:, None, None], (RET_HEADS, dk, dv))
    return decay, jnp.broadcast_to(zeta, (RET_HEADS, RET_CHUNK, dk)), \
        jnp.broadcast_to(xi, (RET_HEADS, RET_CHUNK, dv)), cd


def _ret_mixer(proj, bsz, seq, cos_t, sin_t, gn_w, qk_w, v_w, q_col, lt=512):
    dk = qk_w // RET_HEADS
    dv = v_w // RET_HEADS
    lt = min(lt, seq)
    nc = seq // lt
    decay, zeta, xi, cd = _ret_consts(dk, dv)
    qb = q_col // qk_w
    vb = (q_col + 2 * qk_w) // v_w
    row = lambda b, c: b * nc + c
    c3 = lambda b, c: (0, 0, 0)
    return pl.pallas_call(
        functools.partial(_ret_kernel, dk=dk, dv=dv),
        grid=(bsz, nc),
        in_specs=[
            pl.BlockSpec((lt, qk_w), lambda b, c: (row(b, c), qb)),
            pl.BlockSpec((lt, qk_w), lambda b, c: (row(b, c), qb + 1)),
            pl.BlockSpec((lt, v_w), lambda b, c: (row(b, c), vb)),
            pl.BlockSpec((lt, v_w), lambda b, c: (row(b, c), vb + 1)),
            pl.BlockSpec((lt, LANES), lambda b, c: (row(b, c), 0)),
            pl.BlockSpec((lt, LANES), lambda b, c: (row(b, c), 0)),
            pl.BlockSpec(decay.shape, c3),
            pl.BlockSpec(zeta.shape, c3),
            pl.BlockSpec(xi.shape, c3),
            pl.BlockSpec(cd.shape, c3),
            pl.BlockSpec((1, v_w), lambda b, c: (0, 0)),
        ],
        out_specs=pl.BlockSpec((lt, v_w), lambda b, c: (row(b, c), 0)),
        out_shape=jax.ShapeDtypeStruct((bsz * seq, v_w), F32),
        scratch_shapes=[pltpu.VMEM((RET_HEADS, dk, dv), F32)],
        compiler_params=_params("arbitrary", "arbitrary"),
        name="retention_mixer",
    )(proj, proj, proj, proj, cos_t, sin_t, decay, zeta, xi, cd, gn_w.reshape(1, v_w))


def _lru_kernel(x_ref, gate_ref, cw_ref, cb_ref, wri_ref, bri_ref, sp_ref, o_ref,
                xe_ref, a_ref, b_ref, carry_ref):
    lc, w = x_ref.shape

    @pl.when(pl.program_id(1) == 0)
    def _():
        xe_ref[0:SUBLANES, :] = jnp.zeros((SUBLANES, w), F32)
        carry_ref[...] = jnp.zeros_like(carry_ref)

    x = x_ref[...]
    xe_ref[SUBLANES:, :] = x
    xc = cw_ref[CONV_WIDTH - 1:CONV_WIDTH, :] * x + cb_ref[...]
    for j in range(1, CONV_WIDTH):
        xc = xc + cw_ref[CONV_WIDTH - 1 - j:CONV_WIDTH - j, :] * xe_ref[pl.ds(SUBLANES - j, lc), :]
    xe_ref[0:SUBLANES, :] = x[lc - SUBLANES:, :]

    ri = _sigmoid(jnp.dot(xc.astype(BF16), wri_ref[...], preferred_element_type=F32) + bri_ref[...])
    r = ri[:, :w]
    gi = ri[:, w:]
    log_a = -LRU_C * r * sp_ref[...]
    a = jnp.exp(log_a)
    mult = jnp.sqrt(1.0 - jnp.exp(2.0 * log_a))
    b = mult * (gi * xc)

    rowmod = lax.broadcasted_iota(jnp.int32, (lc, w), 0) % SUBLANES
    for dist in (1, 2, 4):
        keep = rowmod >= dist
        a_sh = jnp.where(keep, pltpu.roll(a, dist, 0), 1.0)
        b_sh = jnp.where(keep, pltpu.roll(b, dist, 0), 0.0)
        b = a * b_sh + b
        a = a * a_sh
    a_ref[...] = a
    b_ref[...] = b

    def body(r8, hprev):
        rows = pl.ds(pl.multiple_of(r8 * SUBLANES, SUBLANES), SUBLANES)
        h = a_ref[rows, :] * hprev + b_ref[rows, :]
        b_ref[rows, :] = h
        return h[SUBLANES - 1:SUBLANES, :]

    carry_ref[...] = lax.fori_loop(0, lc // SUBLANES, body, carry_ref[...])
    o_ref[...] = b_ref[...] * _gelu(gate_ref[...])


def _lru_mixer(proj, bsz, seq, conv_w, conv_b, w_r, b_r, w_i, b_i, lam, w, x_col, lc=512):
    lc = min(lc, seq)
    nc = seq // lc
    eye = jnp.eye(LRU_BLOCKS, dtype=F32)
    dense = lambda m: jnp.einsum('ncd,nm->ncmd', m, eye).reshape(w, w)
    wri = jnp.concatenate([dense(w_r), dense(w_i)], axis=1).astype(BF16)
    bri = jnp.concatenate([b_r, b_i]).reshape(1, 2 * w)
    sp = jax.nn.softplus(-lam.astype(F32)).reshape(1, w)
    xb = x_col // w
    c2 = lambda b, c: (0, 0)
    return pl.pallas_call(
        _lru_kernel,
        grid=(bsz, nc),
        in_specs=[
            pl.BlockSpec((lc, w), lambda b, c: (b * nc + c, xb)),
            pl.BlockSpec((lc, w), lambda b, c: (b * nc + c, xb + 1)),
            pl.BlockSpec((CONV_WIDTH, w), c2),
            pl.BlockSpec((1, w), c2),
            pl.BlockSpec((w, 2 * w), c2),
            pl.BlockSpec((1, 2 * w), c2),
            pl.BlockSpec((1, w), c2),
        ],
        out_specs=pl.BlockSpec((lc, w), lambda b, c: (b * nc + c, 0)),
        out_shape=jax.ShapeDtypeStruct((bsz * seq, w), F32),
        scratch_shapes=[pltpu.VMEM((lc + SUBLANES, w), F32), pltpu.VMEM((lc, w), F32),
                        pltpu.VMEM((lc, w), F32), pltpu.VMEM((1, w), F32)],
        compiler_params=_params("arbitrary", "arbitrary"),
        name="rglru_mixer",
    )(proj, proj, conv_w, conv_b.reshape(1, w), wri, bri, sp)


def _merge_kernel(y0_ref, y1_ref, y2_ref, gl_ref, x_ref, wb_ref, wo_ref, o_ref):
    d = x_ref.shape[1]
    merged = None
    for n, y_ref in enumerate((y0_ref, y1_ref, y2_ref)):
        br = jnp.dot(y_ref[...].astype(BF16), wb_ref[n], preferred_element_type=F32)
        term = _sigmoid(gl_ref[:, n * d:(n + 1) * d]) * br
        merged = term if merged is None else merged + term
    o_ref[...] = x_ref[...] + jnp.dot(merged.astype(BF16), wo_ref[...], preferred_element_type=F32)


def _merge(ys, proj, x2, w_branch, w_out, gate_col, tm=512):
    t, d = x2.shape
    bw = ys[0].shape[1]
    tm = min(tm, t)
    gb = gate_col // (N_BRANCH * d)
    yspec = pl.BlockSpec((tm, bw), lambda i: (i, 0))
    return pl.pallas_call(
        _merge_kernel,
        grid=(t // tm,),
        in_specs=[yspec, yspec, yspec,
                  pl.BlockSpec((tm, N_BRANCH * d), lambda i: (i, gb)),
                  pl.BlockSpec((tm, d), lambda i: (i, 0)),
                  pl.BlockSpec((N_BRANCH, bw, d), lambda i: (0, 0, 0)),
                  pl.BlockSpec((d, d), lambda i: (0, 0))],
        out_specs=pl.BlockSpec((tm, d), lambda i: (i, 0)),
        out_shape=jax.ShapeDtypeStruct((t, d), F32),
        compiler_params=_params("parallel"),
        name="merge_out",
    )(ys[0], ys[1], ys[2], proj, x2, w_branch.astype(BF16), w_out.astype(BF16))


_NEG = -1e30


def _top_rows(s, count):
    rows = []
    for i in range(count):
        m = jnp.max(s, axis=0, keepdims=True)
        rows.append(m)
        if i + 1 < count:
            s = jnp.where(s >= m, _NEG, s)
    return rows


def _peer_kernel(x_ref, nw_ref, wqt_ref, keys_ref, u_ref, vt_ref, fw_ref, o_ref,
                 xn_ref, qt_ref, s2_ref, e2_ref, thr_ref, g_ref, acc_ref, *, final_norm):
    e = pl.program_id(1)
    ne = pl.num_programs(1)
    eb = u_ref.shape[0]
    nk = PEER_NKEYS
    kk = PEER_TOPK + 1

    @pl.when(e == 0)
    def _prep():
        xn = _rms(x_ref[...], nw_ref[...]).astype(BF16)
        xn_ref[...] = xn
        qt_ref[...] = lax.dot_general(wqt_ref[...], xn, (((1,), (1,)), ((), ())),
                                      preferred_element_type=F32)
        acc_ref[...] = jnp.zeros_like(acc_ref)

        def head(h, _):
            q1 = qt_ref[pl.ds(pl.multiple_of(h * 2 * PEER_HALF, PEER_HALF), PEER_HALF), :]
            q2 = qt_ref[pl.ds(pl.multiple_of(h * 2 * PEER_HALF + PEER_HALF, PEER_HALF), PEER_HALF), :]
            s1 = jnp.dot(keys_ref[2 * h], q1, preferred_element_type=F32)
            s2 = jnp.dot(keys_ref[2 * h + 1], q2, preferred_element_type=F32)
            v1 = _top_rows(s1, kk)
            v2 = _top_rows(s2, kk)
            cands = [v1[i] + v2[j] for i in range(kk) for j in range(kk) if (i + 1) * (j + 1) <= kk]
            cand = jnp.concatenate(cands, axis=0)
            top = _top_rows(cand, kk)
            theta = 0.5 * (top[PEER_TOPK - 1] + top[PEER_TOPK])
            cmax = v1[0] + v2[0]
            z = jnp.sum(jnp.where(cand >= theta, jnp.exp(cand - cmax), 0.0), axis=0, keepdims=True)
            s2_ref[h] = s2
            e2_ref[h] = jnp.exp(s2 - v2[0])
            thr_ref[h] = theta - s1
            g_ref[h] = jnp.exp(s1 - v1[0]) / z
            return 0

        lax.fori_loop(0, PEER_HEADS, head, 0)

    act = lax.dot_general(u_ref[...], xn_ref[...], (((1,), (1,)), ((), ())),
                          preferred_element_type=F32)
    parts = []
    for ai in range(eb // nk):
        a = e * (eb // nk) + ai
        wgt = None
        for h in range(PEER_HEADS):
            thr = thr_ref[h, pl.ds(a, 1), :]
            gr = g_ref[h, pl.ds(a, 1), :]
            term = jnp.where(s2_ref[h] >= thr, e2_ref[h] * gr, 0.0)
            wgt = term if wgt is None else wgt + term
        parts.append((_gelu(act[ai * nk:(ai + 1) * nk, :]) * wgt).astype(BF16))
    hid = jnp.concatenate(parts, axis=0)
    acc_ref[...] += jnp.dot(vt_ref[...], hid, preferred_element_type=F32)

    @pl.when(e == ne - 1)
    def _fin():
        out = x_ref[...] + acc_ref[...].T
        if final_norm:
            out = _rms(out, fw_ref[...])
        o_ref[...] = out


def _peer(x2, norm_w, w_query, sub_keys, u_bf16, vt_bf16, final_w, final_norm, tm=512, eb=512):
    t, d = x2.shape
    n_exp = u_bf16.shape[0]
    tm = min(tm, t)
    wqt = w_query.T.astype(BF16)
    keys = sub_keys.reshape(PEER_HEADS * 2, PEER_NKEYS, PEER_HALF).astype(F32)
    c2 = lambda i, e: (0, 0)
    head_buf = pltpu.VMEM((PEER_HEADS, PEER_NKEYS, tm), F32)
    return pl.pallas_call(
        functools.partial(_peer_kernel, final_norm=final_norm),
        grid=(t // tm, n_exp // eb),
        in_specs=[
            pl.BlockSpec((tm, d), lambda i, e: (i, 0)),
            pl.BlockSpec((1, d), c2),
            pl.BlockSpec(wqt.shape, c2),
            pl.BlockSpec(keys.shape, lambda i, e: (0, 0, 0)),
            pl.BlockSpec((eb, d), lambda i, e: (e, 0)),
            pl.BlockSpec((d, eb), lambda i, e: (0, e)),
            pl.BlockSpec((1, d), c2),
        ],
        out_specs=pl.BlockSpec((tm, d), lambda i, e: (i, 0)),
        out_shape=jax.ShapeDtypeStruct((t, d), F32),
        scratch_shapes=[pltpu.VMEM((tm, d), BF16), pltpu.VMEM((wqt.shape[0], tm), F32),
                        head_buf, head_buf, head_buf, head_buf, pltpu.VMEM((d, tm), F32)],
        compiler_params=_params("parallel", "arbitrary"),
        name="peer_dense",
    )(x2, norm_w.reshape(1, d), wqt, keys, u_bf16, vt_bf16, final_w.reshape(1, d))


def kernel(x, positions, norm_mix, w_in, s5_lam_re, s5_lam_im, s5_log_dt, s5_b_re, s5_b_im, s5_c_re, s5_c_im, s5_d, s5_w_glu, s5_b_glu, ret_gn_w, lru_conv_w, lru_conv_b, lru_w_r, lru_b_r, lru_w_i, lru_b_i, lru_lam, w_branch, w_out, norm_ffn, peer_w_query, peer_sub_keys, peer_u, peer_v, final_norm):
    bsz, seq, d = x.shape
    depth = norm_mix.shape[0]
    s5_w = s5_d.shape[1]
    v_w = ret_gn_w.shape[1]
    qk_w = v_w // 2
    lru_w = lru_lam.shape[1]
    q_col = s5_w
    x_col = q_col + 2 * qk_w + 2 * v_w
    gate_col = x_col + 2 * lru_w

    x2 = x.reshape(bsz * seq, d)
    cos_t, sin_t = _rope_tables(positions, qk_w // RET_HEADS // 2)
    for l in range(depth):
        proj = _norm_proj(x2, norm_mix[l], w_in[l].astype(BF16))
        s5c = _s5_consts(s5_lam_re[l], s5_lam_im[l], s5_log_dt[l], s5_b_re[l], s5_b_im[l],
                         s5_c_re[l], s5_c_im[l])
        y_s5 = _s5_mixer(proj, bsz, seq, s5c, s5_d[l], s5_w_glu[l], s5_b_glu[l])
        y_ret = _ret_mixer(proj, bsz, seq, cos_t, sin_t, ret_gn_w[l], qk_w, v_w, q_col)
        y_lru = _lru_mixer(proj, bsz, seq, lru_conv_w[l], lru_conv_b[l], lru_w_r[l], lru_b_r[l],
                           lru_w_i[l], lru_b_i[l], lru_lam[l], lru_w, x_col)
        x2 = _merge((y_s5, y_ret, y_lru), proj, x2, w_branch[l], w_out[l], gate_col)
        x2 = _peer(x2, norm_ffn[l], peer_w_query[l], peer_sub_keys[l], peer_u[l].astype(BF16),
                   peer_v[l].T.astype(BF16), final_norm, final_norm=(l == depth - 1))
    return x2.reshape(bsz, seq, d)
```

```python
import functools
import math

import jax
import jax.numpy as jnp
import numpy as np
from jax import lax
from jax.experimental import pallas as pl
from jax.experimental.pallas import tpu as pltpu

F32 = jnp.float32
BF16 = jnp.bfloat16

EPS = 1e-6
S5_GROUP = 16
S5_STATE = 64
RET_HEADS = 4
RET_CHUNK = 128
ROPE_BASE = 10000.0
LRU_BLOCKS = 8
CONV_WIDTH = 4
LRU_C = 8.0
N_BRANCH = 3
PEER_HEADS = 8
PEER_NKEYS = 128
PEER_TOPK = 16
PEER_HALF = 64

SUBLANES = 8
LANES = 128
VMEM_LIMIT = 56 * 1024 * 1024

_SQRT_2_OVER_PI = math.sqrt(2.0 / math.pi)


def _gelu(x):
    return 0.5 * x * (1.0 + jnp.tanh(_SQRT_2_OVER_PI * (x + 0.044715 * (x * x * x))))


def _sigmoid(x):
    return 1.0 / (1.0 + jnp.exp(-x))


def _rms(x, w):
    return x * lax.rsqrt(jnp.mean(x * x, axis=-1, keepdims=True) + EPS) * w


def _params(*sem):
    return pltpu.CompilerParams(dimension_semantics=sem, vmem_limit_bytes=VMEM_LIMIT)


def _norm_proj_kernel(x_ref, nw_ref, w_ref, o_ref, xn_ref):
    @pl.when(pl.program_id(1) == 0)
    def _():
        xn_ref[...] = _rms(x_ref[...], nw_ref[...]).astype(BF16)

    o_ref[...] = jnp.dot(xn_ref[...], w_ref[...], preferred_element_type=F32)


def _norm_proj(x2, norm_w, w_bf16, tm=1024, tn=1024):
    t, d = x2.shape
    n = w_bf16.shape[1]
    tm = min(tm, t)
    return pl.pallas_call(
        _norm_proj_kernel,
        grid=(t // tm, n // tn),
        in_specs=[
            pl.BlockSpec((tm, d), lambda i, j: (i, 0)),
            pl.BlockSpec((1, d), lambda i, j: (0, 0)),
            pl.BlockSpec((d, tn), lambda i, j: (0, j)),
        ],
        out_specs=pl.BlockSpec((tm, tn), lambda i, j: (i, j)),
        out_shape=jax.ShapeDtypeStruct((t, n), F32),
        scratch_shapes=[pltpu.VMEM((tm, d), BF16)],
        compiler_params=_params("parallel", "arbitrary"),
        name="norm_proj",
    )(x2, norm_w.reshape(1, d), w_bf16)


S5_LANE_CHUNK = 512


def _s5_kernel(u_ref, bmat_ref, cmat_ref, sc_ref, d_ref, wglu_ref, bglu_ref, o_ref,
               bu_ref, carry_ref, *, ns):
    @pl.when(pl.program_id(1) == 0)
    def _():
        carry_ref[...] = jnp.zeros_like(carry_ref)

    u = u_ref[...]
    lc = u.shape[0]
    bu_ref[...] = jnp.dot(u.astype(BF16), bmat_ref[...], preferred_element_type=F32)

    for c0 in range(0, ns, S5_LANE_CHUNK):
        re_sl = pl.ds(c0, S5_LANE_CHUNK)
        im_sl = pl.ds(ns + c0, S5_LANE_CHUNK)
        consts = [sc_ref[k, :, c0:c0 + S5_LANE_CHUNK] for k in range(8)]

        def body(r, carry, re_sl=re_sl, im_sl=im_sl, consts=consts):
            hpr, hpi = carry
            rows = pl.ds(pl.multiple_of(r * SUBLANES, SUBLANES), SUBLANES)
            br = bu_ref[rows, re_sl]
            bi = bu_ref[rows, im_sl]
            for lvl, dist in enumerate((1, 2, 4)):
                ar, ai = consts[2 * lvl], consts[2 * lvl + 1]
                sr = pltpu.roll(br, dist, 0)
                si = pltpu.roll(bi, dist, 0)
                br, bi = br + ar * sr - ai * si, bi + ar * si + ai * sr
            pr, pim = consts[6], consts[7]
            hr = br + pr * hpr - pim * hpi
            hi = bi + pr * hpi + pim * hpr
            bu_ref[rows, re_sl] = hr
            bu_ref[rows, im_sl] = hi
            return hr[SUBLANES - 1:SUBLANES, :], hi[SUBLANES - 1:SUBLANES, :]

        hr_last, hi_last = lax.fori_loop(
            0, lc // SUBLANES, body, (carry_ref[:, re_sl], carry_ref[:, im_sl]))
        carry_ref[:, re_sl] = hr_last
        carry_ref[:, im_sl] = hi_last

    y = jnp.dot(bu_ref[...].astype(BF16), cmat_ref[...], preferred_element_type=F32)
    z = _gelu(y + d_ref[...] * u)
    gl = jnp.dot(z.astype(BF16), wglu_ref[...], preferred_element_type=F32) + bglu_ref[...]
    o_ref[...] = z * _sigmoid(gl)


def _s5_consts(lam_re, lam_im, log_dt, b_re, b_im, c_re, c_im):
    g, p = lam_re.shape
    h16 = b_re.shape[-1]
    lr, li = lam_re.astype(F32), lam_im.astype(F32)
    dt = jnp.broadcast_to(jnp.exp(log_dt.astype(F32))[:, None], lr.shape)

    def lam_bar_pow(lr, li, dt, n):
        mag = jnp.exp(lr * dt * n)
        return mag * jnp.cos(li * dt * n), mag * jnp.sin(li * dt * n)

    br_, bi_ = lam_bar_pow(lr, li, dt, 1.0)
    den = lr * lr + li * li
    cr = ((br_ - 1.0) * lr + bi_ * li) / den
    ci = (bi_ * lr - (br_ - 1.0) * li) / den
    bre, bim = b_re.astype(F32), b_im.astype(F32)
    bbar_re = cr[..., None] * bre - ci[..., None] * bim
    bbar_im = cr[..., None] * bim + ci[..., None] * bre
    eye = jnp.eye(g, dtype=F32)

    def bdiag_in(m):
        return jnp.einsum('gph,gk->ghkp', m, eye).reshape(g * h16, g * p)

    def bdiag_out(m):
        return jnp.einsum('ghp,gk->gpkh', m, eye).reshape(g * p, g * h16)

    bmat = jnp.concatenate([bdiag_in(bbar_re), bdiag_in(bbar_im)], axis=1)
    cmat = jnp.concatenate([bdiag_out(c_re.astype(F32)), bdiag_out(-c_im.astype(F32))], axis=0)

    row = jnp.arange(SUBLANES, dtype=F32)[:, None]
    planes = []
    for dist in (1, 2, 4):
        pr, pi_ = lam_bar_pow(lr, li, dt, float(dist))
        keep = row >= dist
        planes += [jnp.where(keep, pr.reshape(1, g * p), 0.0), jnp.where(keep, pi_.reshape(1, g * p), 0.0)]
    flat = lambda m: m.reshape(1, g * p)
    pr, pi_ = lam_bar_pow(flat(lr), flat(li), flat(dt), row + 1.0)
    planes += [pr, pi_]
    return bmat.astype(BF16), cmat.astype(BF16), jnp.stack(planes).astype(F32)


def _s5_mixer(proj, bsz, seq, consts, d_skip, w_glu, b_glu, lc=256):
    bmat, cmat, sc = consts
    w = bmat.shape[0]
    ns = bmat.shape[1] // 2
    lc = min(lc, seq)
    nc = seq // lc
    const2 = lambda b, c: (0, 0)
    return pl.pallas_call(
        functools.partial(_s5_kernel, ns=ns),
        grid=(bsz, nc),
        in_specs=[
            pl.BlockSpec((lc, w), lambda b, c: (b * nc + c, 0)),
            pl.BlockSpec(bmat.shape, const2),
            pl.BlockSpec(cmat.shape, const2),
            pl.BlockSpec(sc.shape, lambda b, c: (0, 0, 0)),
            pl.BlockSpec((1, w), const2),
            pl.BlockSpec((w, w), const2),
            pl.BlockSpec((1, w), const2),
        ],
        out_specs=pl.BlockSpec((lc, w), lambda b, c: (b * nc + c, 0)),
        out_shape=jax.ShapeDtypeStruct((bsz * seq, w), F32),
        scratch_shapes=[pltpu.VMEM((lc, 2 * ns), F32), pltpu.VMEM((1, 2 * ns), F32)],
        compiler_params=_params("arbitrary", "arbitrary"),
        name="s5_mixer",
    )(proj, bmat, cmat, sc, d_skip.reshape(1, w), w_glu.astype(BF16), b_glu.reshape(1, w))


def _rope_kernel(pos_ref, freq_ref, cos_ref, sin_ref):
    ang = pos_ref[...].astype(F32) * freq_ref[...]
    cos_ref[...] = jnp.cos(ang)
    sin_ref[...] = jnp.sin(ang)


def _rope_tables(positions, half, tm=2048):
    t = positions.size
    tm = min(tm, t)
    reps = LANES // half
    inv_freq = ROPE_BASE ** (-jnp.arange(half, dtype=F32) / half)
    freq = jnp.tile(inv_freq, reps).reshape(1, LANES)
    return pl.pallas_call(
        _rope_kernel,
        grid=(t // tm,),
        in_specs=[pl.BlockSpec((tm, 1), lambda i: (i, 0)), pl.BlockSpec((1, LANES), lambda i: (0, 0))],
        out_specs=[pl.BlockSpec((tm, LANES), lambda i: (i, 0))] * 2,
        out_shape=[jax.ShapeDtypeStruct((t, LANES), F32)] * 2,
        compiler_params=_params("parallel"),
        name="rope_tables",
    )(positions.reshape(t, 1), freq)


def _ret_kernel(q_ref, k_ref, v_ref, g_ref, cos_ref, sin_ref, dec_ref, zeta_ref, xi_ref, cd_ref,
                gnw_ref, o_ref, state_ref, *, dk, dv):
    @pl.when(pl.program_id(1) == 0)
    def _():
        state_ref[...] = jnp.zeros_like(state_ref)

    lt = q_ref.shape[0]
    half = dk // 2
    qkw = RET_HEADS * dk
    cos = jnp.concatenate([cos_ref[...]] * (qkw // LANES), axis=1)
    sin = jnp.concatenate([sin_ref[...]] * (qkw // LANES), axis=1)
    lane = lax.broadcasted_iota(jnp.int32, (lt, qkw), 1)
    first = (lane % dk) < half
    sin_signed = jnp.where(first, -sin, sin)

    def rot(x):
        swapped = jnp.where(first, pltpu.roll(x, qkw - half, 1), pltpu.roll(x, half, 1))
        return x * cos + swapped * sin_signed

    q = rot(q_ref[...])
    k = rot(k_ref[...]) * (dk ** -0.5)
    v = v_ref[...]
    g = g_ref[...]
    gnw = gnw_ref[...]

    for c0 in range(0, lt, RET_CHUNK):
        for h in range(RET_HEADS):
            qh = q[c0:c0 + RET_CHUNK, h * dk:(h + 1) * dk]
            kh = k[c0:c0 + RET_CHUNK, h * dk:(h + 1) * dk]
            vh = v[c0:c0 + RET_CHUNK, h * dv:(h + 1) * dv]
            st = state_ref[h]
            scores = lax.dot_general(qh, kh, (((1,), (1,)), ((), ())),
                                     preferred_element_type=F32) * dec_ref[h]
            o = jnp.dot(scores, vh, preferred_element_type=F32)
            o = o + jnp.dot(qh, st, preferred_element_type=F32) * xi_ref[h]
            kz = kh * zeta_ref[h]
            kv = lax.dot_general(kz, vh, (((0,), (0,)), ((), ())), preferred_element_type=F32)
            state_ref[h] = cd_ref[h] * st + kv
            mu = jnp.mean(o, axis=-1, keepdims=True)
            oc = o - mu
            var = jnp.mean(oc * oc, axis=-1, keepdims=True)
            on = oc * lax.rsqrt(var + EPS) * gnw[:, h * dv:(h + 1) * dv]
            gh = g[c0:c0 + RET_CHUNK, h * dv:(h + 1) * dv]
            o_ref[c0:c0 + RET_CHUNK, h * dv:(h + 1) * dv] = gh * _sigmoid(gh) * on


def _ret_consts(dk, dv):
    log_gamma = jnp.log1p(-(2.0 ** (-5.0 - jnp.arange(RET_HEADS, dtype=F32))))
    idx = jnp.arange(RET_CHUNK, dtype=F32)
    diff = idx[:, None] - idx[None, :]
    decay = jnp.where(diff >= 0, jnp.exp(log_gamma[:, None, None] * jnp.maximum(diff, 0.0)), 0.0)
    zeta = jnp.exp(log_gamma[:, None] * (RET_CHUNK - 1 - idx))[:, :, None]
    xi = jnp.exp(log_gamma[:, None] * (idx + 1.0))[:, :, None]
    cd = jnp.broadcast_to(jnp.exp(log_gamma * RET_CHUNK)[:, None, None], (RET_HEADS, dk, dv))
    return decay, jnp.broadcast_to(zeta, (RET_HEADS, RET_CHUNK, dk)), \
        jnp.broadcast_to(xi, (RET_HEADS, RET_CHUNK, dv)), cd


def _ret_mixer(proj, bsz, seq, cos_t, sin_t, gn_w, qk_w, v_w, q_col, lt=512):
    dk = qk_w // RET_HEADS
    dv = v_w // RET_HEADS
    lt = min(lt, seq)
    nc = seq // lt
    decay, zeta, xi, cd = _ret_consts(dk, dv)
    qb = q_col // qk_w
    vb = (q_col + 2 * qk_w) // v_w
    row = lambda b, c: b * nc + c
    c3 = lambda b, c: (0, 0, 0)
    return pl.pallas_call(
        functools.partial(_ret_kernel, dk=dk, dv=dv),
        grid=(bsz, nc),
        in_specs=[
            pl.BlockSpec((lt, qk_w), lambda b, c: (row(b, c), qb)),
            pl.BlockSpec((lt, qk_w), lambda b, c: (row(b, c), qb + 1)),
            pl.BlockSpec((lt, v_w), lambda b, c: (row(b, c), vb)),
            pl.BlockSpec((lt, v_w), lambda b, c: (row(b, c), vb + 1)),
            pl.BlockSpec((lt, LANES), lambda b, c: (row(b, c), 0)),
            pl.BlockSpec((lt, LANES), lambda b, c: (row(b, c), 0)),
            pl.BlockSpec(decay.shape, c3),
            pl.BlockSpec(zeta.shape, c3),
            pl.BlockSpec(xi.shape, c3),
            pl.BlockSpec(cd.shape, c3),
            pl.BlockSpec((1, v_w), lambda b, c: (0, 0)),
        ],
        out_specs=pl.BlockSpec((lt, v_w), lambda b, c: (row(b, c), 0)),
        out_shape=jax.ShapeDtypeStruct((bsz * seq, v_w), F32),
        scratch_shapes=[pltpu.VMEM((RET_HEADS, dk, dv), F32)],
        compiler_params=_params("arbitrary", "arbitrary"),
        name="retention_mixer",
    )(proj, proj, proj, proj, cos_t, sin_t, decay, zeta, xi, cd, gn_w.reshape(1, v_w))


def _lru_kernel(x_ref, gate_ref, cw_ref, cb_ref, wri_ref, bri_ref, sp_ref, o_ref,
                xe_ref, a_ref, b_ref, carry_ref):
    lc, w = x_ref.shape

    @pl.when(pl.program_id(1) == 0)
    def _():
        xe_ref[0:SUBLANES, :] = jnp.zeros((SUBLANES, w), F32)
        carry_ref[...] = jnp.zeros_like(carry_ref)

    x = x_ref[...]
    xe_ref[SUBLANES:, :] = x
    xc = cw_ref[CONV_WIDTH - 1:CONV_WIDTH, :] * x + cb_ref[...]
    for j in range(1, CONV_WIDTH):
        xc = xc + cw_ref[CONV_WIDTH - 1 - j:CONV_WIDTH - j, :] * xe_ref[pl.ds(SUBLANES - j, lc), :]
    xe_ref[0:SUBLANES, :] = x[lc - SUBLANES:, :]

    ri = _sigmoid(jnp.dot(xc.astype(BF16), wri_ref[...], preferred_element_type=F32) + bri_ref[...])
    r = ri[:, :w]
    gi = ri[:, w:]
    log_a = -LRU_C * r * sp_ref[...]
    a = jnp.exp(log_a)
    mult = jnp.sqrt(1.0 - jnp.exp(2.0 * log_a))
    b = mult * (gi * xc)

    rowmod = lax.broadcasted_iota(jnp.int32, (lc, w), 0) % SUBLANES
    for dist in (1, 2, 4):
        keep = rowmod >= dist
        a_sh = jnp.where(keep, pltpu.roll(a, dist, 0), 1.0)
        b_sh = jnp.where(keep, pltpu.roll(b, dist, 0), 0.0)
        b = a * b_sh + b
        a = a * a_sh
    a_ref[...] = a
    b_ref[...] = b

    def body(r8, hprev):
        rows = pl.ds(pl.multiple_of(r8 * SUBLANES, SUBLANES), SUBLANES)
        h = a_ref[rows, :] * hprev + b_ref[rows, :]
        b_ref[rows, :] = h
        return h[SUBLANES - 1:SUBLANES, :]

    carry_ref[...] = lax.fori_loop(0, lc // SUBLANES, body, carry_ref[...])
    o_ref[...] = b_ref[...] * _gelu(gate_ref[...])


def _lru_mixer(proj, bsz, seq, conv_w, conv_b, w_r, b_r, w_i, b_i, lam, w, x_col, lc=512):
    lc = min(lc, seq)
    nc = seq // lc
    eye = jnp.eye(LRU_BLOCKS, dtype=F32)
    dense = lambda m: jnp.einsum('ncd,nm->ncmd', m, eye).reshape(w, w)
    wri = jnp.concatenate([dense(w_r), dense(w_i)], axis=1).astype(BF16)
    bri = jnp.concatenate([b_r, b_i]).reshape(1, 2 * w)
    sp = jax.nn.softplus(-lam.astype(F32)).reshape(1, w)
    xb = x_col // w
    c2 = lambda b, c: (0, 0)
    return pl.pallas_call(
        _lru_kernel,
        grid=(bsz, nc),
        in_specs=[
            pl.BlockSpec((lc, w), lambda b, c: (b * nc + c, xb)),
            pl.BlockSpec((lc, w), lambda b, c: (b * nc + c, xb + 1)),
            pl.BlockSpec((CONV_WIDTH, w), c2),
            pl.BlockSpec((1, w), c2),
            pl.BlockSpec((w, 2 * w), c2),
            pl.BlockSpec((1, 2 * w), c2),
            pl.BlockSpec((1, w), c2),
        ],
        out_specs=pl.BlockSpec((lc, w), lambda b, c: (b * nc + c, 0)),
        out_shape=jax.ShapeDtypeStruct((bsz * seq, w), F32),
        scratch_shapes=[pltpu.VMEM((lc + SUBLANES, w), F32), pltpu.VMEM((lc, w), F32),
                        pltpu.VMEM((lc, w), F32), pltpu.VMEM((1, w), F32)],
        compiler_params=_params("arbitrary", "arbitrary"),
        name="rglru_mixer",
    )(proj, proj, conv_w, conv_b.reshape(1, w), wri, bri, sp)


def _merge_kernel(y0_ref, y1_ref, y2_ref, gl_ref, x_ref, wb_ref, wo_ref, o_ref):
    d = x_ref.shape[1]
    merged = None
    for n, y_ref in enumerate((y0_ref, y1_ref, y2_ref)):
        br = jnp.dot(y_ref[...].astype(BF16), wb_ref[n], preferred_element_type=F32)
        term = _sigmoid(gl_ref[:, n * d:(n + 1) * d]) * br
        merged = term if merged is None else merged + term
    o_ref[...] = x_ref[...] + jnp.dot(merged.astype(BF16), wo_ref[...], preferred_element_type=F32)


def _merge(ys, proj, x2, w_branch, w_out, gate_col, tm=512):
    t, d = x2.shape
    bw = ys[0].shape[1]
    tm = min(tm, t)
    gb = gate_col // (N_BRANCH * d)
    yspec = pl.BlockSpec((tm, bw), lambda i: (i, 0))
    return pl.pallas_call(
        _merge_kernel,
        grid=(t // tm,),
        in_specs=[yspec, yspec, yspec,
                  pl.BlockSpec((tm, N_BRANCH * d), lambda i: (i, gb)),
                  pl.BlockSpec((tm, d), lambda i: (i, 0)),
                  pl.BlockSpec((N_BRANCH, bw, d), lambda i: (0, 0, 0)),
                  pl.BlockSpec((d, d), lambda i: (0, 0))],
        out_specs=pl.BlockSpec((tm, d), lambda i: (i, 0)),
        out_shape=jax.ShapeDtypeStruct((t, d), F32),
        compiler_params=_params("parallel"),
        name="merge_out",
    )(ys[0], ys[1], ys[2], proj, x2, w_branch.astype(BF16), w_out.astype(BF16))


_NEG = -1e30
PEER_SUB = 256
BF16_ROWS = 16


def _top_rows(s, count, n_ranked=0):
    rows = []
    rank = None
    if n_ranked:
        rank = jnp.full(s.shape, float(n_ranked), F32)
    for i in range(count):
        m = jnp.max(s, axis=0, keepdims=True)
        rows.append(m)
        if i + 1 < count or i < n_ranked:
            hit = s >= m
            if i < n_ranked:
                rank = jnp.where(hit, float(i), rank)
            s = jnp.where(hit, _NEG, s)
    return (rows, rank) if n_ranked else rows


def _gelu_gate(x):
    k1 = -2.0 * _SQRT_2_OVER_PI
    k3 = k1 * 0.044715
    return x / (1.0 + jnp.exp(x * (k1 + k3 * (x * x))))


def _peer_kernel(x_ref, nw_ref, wqt_ref, keys_ref, u_ref, vt_ref, fw_ref, o_ref,
                 xn_ref, qt_ref, rank_ref, e2_ref, cnt_ref, g_ref, acc_ref, *, final_norm):
    e = pl.program_id(1)
    ne = pl.num_programs(1)
    eb = u_ref.shape[0]
    tm = x_ref.shape[0]
    nk = PEER_NKEYS
    kk = PEER_TOPK + 1
    ntile = nk // BF16_ROWS

    @pl.when(e == 0)
    def _prep():
        xn = _rms(x_ref[...], nw_ref[...]).astype(BF16)
        xn_ref[...] = xn
        qt_ref[...] = lax.dot_general(wqt_ref[...], xn, (((1,), (1,)), ((), ())),
                                      preferred_element_type=F32)
        acc_ref[...] = jnp.zeros_like(acc_ref)

        def head(h, _):
            q1 = qt_ref[pl.ds(pl.multiple_of(h * 2 * PEER_HALF, PEER_HALF), PEER_HALF), :]
            q2 = qt_ref[pl.ds(pl.multiple_of(h * 2 * PEER_HALF + PEER_HALF, PEER_HALF), PEER_HALF), :]
            s1 = jnp.dot(keys_ref[2 * h], q1, preferred_element_type=F32)
            s2 = jnp.dot(keys_ref[2 * h + 1], q2, preferred_element_type=F32)
            v1 = _top_rows(s1, kk)
            v2, rank2 = _top_rows(s2, kk, PEER_TOPK)
            cands = [v1[i] + v2[j] for i in range(kk) for j in range(kk) if (i + 1) * (j + 1) <= kk]
            cand = jnp.concatenate(cands, axis=0)
            top = _top_rows(cand, kk)
            theta = 0.5 * (top[PEER_TOPK - 1] + top[PEER_TOPK])
            cmax = v1[0] + v2[0]
            z = jnp.sum(jnp.where(cand >= theta, jnp.exp(cand - cmax), 0.0), axis=0, keepdims=True)
            thr = theta - s1
            cnt = jnp.zeros_like(s1)
            for j in range(PEER_TOPK):
                cnt = cnt + jnp.where(v2[j] >= thr, 1.0, 0.0)
            e2 = jnp.exp(s2 - v2[0])
            for r in range(ntile):
                rs = slice(r * BF16_ROWS, (r + 1) * BF16_ROWS)
                rank_ref[h, r] = rank2[rs, :].astype(BF16)
                e2_ref[h, r] = e2[rs, :].astype(BF16)
            cnt_ref[h] = cnt
            g_ref[h] = jnp.exp(s1 - v1[0]) / z
            return 0

        lax.fori_loop(0, PEER_HEADS, head, 0)

    nsub = eb // PEER_SUB

    def score(j):
        return lax.dot_general(u_ref[j * PEER_SUB:(j + 1) * PEER_SUB, :], xn_ref[...],
                               (((1,), (1,)), ((), ())), preferred_element_type=F32)

    act_next = score(0)
    for j in range(nsub):
        rows = slice(j * PEER_SUB, (j + 1) * PEER_SUB)
        act = act_next
        if j + 1 < nsub:
            act_next = score(j + 1)
        parts = []
        for ai in range(PEER_SUB // nk):
            a = e * (eb // nk) + j * (PEER_SUB // nk) + ai
            wgt = None
            for h in range(PEER_HEADS):
                cnt = jnp.broadcast_to(cnt_ref[h, pl.ds(a, 1), :], (BF16_ROWS, tm)).astype(BF16)
                gr = jnp.broadcast_to(g_ref[h, pl.ds(a, 1), :], (BF16_ROWS, tm)).astype(BF16)
                term = jnp.where(rank_ref[h] < cnt[None], e2_ref[h] * gr[None], 0.0)
                wgt = term if wgt is None else wgt + term
            gel = _gelu_gate(act[ai * nk:(ai + 1) * nk, :]).astype(BF16)
            for r in range(ntile):
                parts.append(gel[r * BF16_ROWS:(r + 1) * BF16_ROWS, :] * wgt[r])
        hid = jnp.concatenate(parts, axis=0)
        acc_ref[...] += jnp.dot(vt_ref[:, rows], hid, preferred_element_type=F32)

    @pl.when(e == ne - 1)
    def _fin():
        out = x_ref[...] + acc_ref[...].T
        if final_norm:
            out = _rms(out, fw_ref[...])
        o_ref[...] = out


def _peer(x2, norm_w, w_query, sub_keys, u_bf16, vt_bf16, final_w, final_norm, tm=512, eb=1024):
    t, d = x2.shape
    n_exp = u_bf16.shape[0]
    tm = min(tm, t)
    wqt = w_query.T.astype(BF16)
    keys = sub_keys.reshape(PEER_HEADS * 2, PEER_NKEYS, PEER_HALF).astype(F32)
    c2 = lambda i, e: (0, 0)
    head_buf = pltpu.VMEM((PEER_HEADS, PEER_NKEYS, tm), F32)
    packed_buf = pltpu.VMEM((PEER_HEADS, PEER_NKEYS // BF16_ROWS, BF16_ROWS, tm), BF16)
    return pl.pallas_call(
        functools.partial(_peer_kernel, final_norm=final_norm),
        grid=(t // tm, n_exp // eb),
        in_specs=[
            pl.BlockSpec((tm, d), lambda i, e: (i, 0)),
            pl.BlockSpec((1, d), c2),
            pl.BlockSpec(wqt.shape, c2),
            pl.BlockSpec(keys.shape, lambda i, e: (0, 0, 0)),
            pl.BlockSpec((eb, d), lambda i, e: (e, 0)),
            pl.BlockSpec((d, eb), lambda i, e: (0, e)),
            pl.BlockSpec((1, d), c2),
        ],
        out_specs=pl.BlockSpec((tm, d), lambda i, e: (i, 0)),
        out_shape=jax.ShapeDtypeStruct((t, d), F32),
        scratch_shapes=[pltpu.VMEM((tm, d), BF16), pltpu.VMEM((wqt.shape[0], tm), F32),
                        packed_buf, packed_buf, head_buf, head_buf, pltpu.VMEM((d, tm), F32)],
        compiler_params=_params("parallel", "arbitrary"),
        name="peer_dense",
    )(x2, norm_w.reshape(1, d), wqt, keys, u_bf16, vt_bf16, final_w.reshape(1, d))


def kernel(x, positions, norm_mix, w_in, s5_lam_re, s5_lam_im, s5_log_dt, s5_b_re, s5_b_im, s5_c_re, s5_c_im, s5_d, s5_w_glu, s5_b_glu, ret_gn_w, lru_conv_w, lru_conv_b, lru_w_r, lru_b_r, lru_w_i, lru_b_i, lru_lam, w_branch, w_out, norm_ffn, peer_w_query, peer_sub_keys, peer_u, peer_v, final_norm):
    bsz, seq, d = x.shape
    depth = norm_mix.shape[0]
    s5_w = s5_d.shape[1]
    v_w = ret_gn_w.shape[1]
    qk_w = v_w // 2
    lru_w = lru_lam.shape[1]
    q_col = s5_w
    x_col = q_col + 2 * qk_w + 2 * v_w
    gate_col = x_col + 2 * lru_w

    x2 = x.reshape(bsz * seq, d)
    cos_t, sin_t = _rope_tables(positions, qk_w // RET_HEADS // 2)
    for l in range(depth):
        proj = _norm_proj(x2, norm_mix[l], w_in[l].astype(BF16))
        s5c = _s5_consts(s5_lam_re[l], s5_lam_im[l], s5_log_dt[l], s5_b_re[l], s5_b_im[l],
                         s5_c_re[l], s5_c_im[l])
        y_s5 = _s5_mixer(proj, bsz, seq, s5c, s5_d[l], s5_w_glu[l], s5_b_glu[l])
        y_ret = _ret_mixer(proj, bsz, seq, cos_t, sin_t, ret_gn_w[l], qk_w, v_w, q_col)
        y_lru = _lru_mixer(proj, bsz, seq, lru_conv_w[l], lru_conv_b[l], lru_w_r[l], lru_b_r[l],
                           lru_w_i[l], lru_b_i[l], lru_lam[l], lru_w, x_col)
        x2 = _merge((y_s5, y_ret, y_lru), proj, x2, w_branch[l], w_out[l], gate_col)
        x2 = _peer(x2, norm_ffn[l], peer_w_query[l], peer_sub_keys[l], peer_u[l].astype(BF16),
                   peer_v[l].T.astype(BF16), final_norm, final_norm=(l == depth - 1))
    return x2.reshape(bsz, seq, d)
```

```python
import functools
import math

import jax
import jax.numpy as jnp
import numpy as np
from jax import lax
from jax.experimental import pallas as pl
from jax.experimental.pallas import tpu as pltpu

F32 = jnp.float32
BF16 = jnp.bfloat16

EPS = 1e-6
S5_GROUP = 16
S5_STATE = 64
RET_HEADS = 4
RET_CHUNK = 128
ROPE_BASE = 10000.0
LRU_BLOCKS = 8
CONV_WIDTH = 4
LRU_C = 8.0
N_BRANCH = 3
PEER_HEADS = 8
PEER_NKEYS = 128
PEER_TOPK = 16
PEER_HALF = 64

SUBLANES = 8
LANES = 128
VMEM_LIMIT = 56 * 1024 * 1024

_SQRT_2_OVER_PI = math.sqrt(2.0 / math.pi)


def _gelu(x):
    return 0.5 * x * (1.0 + jnp.tanh(_SQRT_2_OVER_PI * (x + 0.044715 * (x * x * x))))


def _sigmoid(x):
    return 1.0 / (1.0 + jnp.exp(-x))


def _rms(x, w):
    return x * lax.rsqrt(jnp.mean(x * x, axis=-1, keepdims=True) + EPS) * w


def _params(*sem, flags=None):
    return pltpu.CompilerParams(dimension_semantics=sem, vmem_limit_bytes=VMEM_LIMIT, flags=flags)


def _norm_proj_kernel(x_ref, nw_ref, w_ref, o_ref, xn_ref):
    @pl.when(pl.program_id(1) == 0)
    def _():
        xn_ref[...] = _rms(x_ref[...], nw_ref[...]).astype(BF16)

    o_ref[...] = jnp.dot(xn_ref[...], w_ref[...], preferred_element_type=F32)


def _norm_proj(x2, norm_w, w_bf16, tm=1024, tn=1024):
    t, d = x2.shape
    n = w_bf16.shape[1]
    tm = min(tm, t)
    return pl.pallas_call(
        _norm_proj_kernel,
        grid=(t // tm, n // tn),
        in_specs=[
            pl.BlockSpec((tm, d), lambda i, j: (i, 0)),
            pl.BlockSpec((1, d), lambda i, j: (0, 0)),
            pl.BlockSpec((d, tn), lambda i, j: (0, j)),
        ],
        out_specs=pl.BlockSpec((tm, tn), lambda i, j: (i, j)),
        out_shape=jax.ShapeDtypeStruct((t, n), F32),
        scratch_shapes=[pltpu.VMEM((tm, d), BF16)],
        compiler_params=_params("parallel", "arbitrary"),
        name="norm_proj",
    )(x2, norm_w.reshape(1, d), w_bf16)


S5_LANE_CHUNK = 512
S5_SPLIT = 2


def _s5_kernel(u_ref, bmat_ref, cmat_ref, sc_ref, d_ref, wglu_ref, bglu_ref, o_ref,
               bu_ref, carry_ref, *, ns):
    nh = ns // S5_SPLIT

    @pl.when(pl.program_id(1) == 0)
    def _():
        carry_ref[...] = jnp.zeros_like(carry_ref)

    u = u_ref[...]
    lc, w = u.shape
    ub = u.astype(BF16)
    for hh in range(S5_SPLIT):
        part = jnp.dot(ub[:, hh * (w // S5_SPLIT):(hh + 1) * (w // S5_SPLIT)], bmat_ref[hh],
                       preferred_element_type=F32)
        bu_ref[:, hh * nh:(hh + 1) * nh] = part[:, :nh]
        bu_ref[:, ns + hh * nh:ns + (hh + 1) * nh] = part[:, nh:]

    for c0 in range(0, ns, S5_LANE_CHUNK):
        re_sl = pl.ds(c0, S5_LANE_CHUNK)
        im_sl = pl.ds(ns + c0, S5_LANE_CHUNK)
        consts = [sc_ref[k, :, c0:c0 + S5_LANE_CHUNK] for k in range(8)]

        def body(r, carry, re_sl=re_sl, im_sl=im_sl, consts=consts):
            hpr, hpi = carry
            rows = pl.ds(pl.multiple_of(r * SUBLANES, SUBLANES), SUBLANES)
            br = bu_ref[rows, re_sl]
            bi = bu_ref[rows, im_sl]
            for lvl, dist in enumerate((1, 2, 4)):
                ar, ai = consts[2 * lvl], consts[2 * lvl + 1]
                sr = pltpu.roll(br, dist, 0)
                si = pltpu.roll(bi, dist, 0)
                br, bi = br + ar * sr - ai * si, bi + ar * si + ai * sr
            pr, pim = consts[6], consts[7]
            hr = br + pr * hpr - pim * hpi
            hi = bi + pr * hpi + pim * hpr
            bu_ref[rows, re_sl] = hr
            bu_ref[rows, im_sl] = hi
            return hr[SUBLANES - 1:SUBLANES, :], hi[SUBLANES - 1:SUBLANES, :]

        hr_last, hi_last = lax.fori_loop(
            0, lc // SUBLANES, body, (carry_ref[:, re_sl], carry_ref[:, im_sl]))
        carry_ref[:, re_sl] = hr_last
        carry_ref[:, im_sl] = hi_last

    ys = []
    for hh in range(S5_SPLIT):
        hs = jnp.concatenate([bu_ref[:, hh * nh:(hh + 1) * nh],
                              bu_ref[:, ns + hh * nh:ns + (hh + 1) * nh]], axis=1)
        ys.append(jnp.dot(hs.astype(BF16), cmat_ref[hh], preferred_element_type=F32))
    y = jnp.concatenate(ys, axis=1)
    z = _gelu(y + d_ref[...] * u)
    gl = jnp.dot(z.astype(BF16), wglu_ref[...], preferred_element_type=F32) + bglu_ref[...]
    o_ref[...] = z * _sigmoid(gl)


def _s5_consts(lam_re, lam_im, log_dt, b_re, b_im, c_re, c_im):
    g, p = lam_re.shape
    h16 = b_re.shape[-1]
    lr, li = lam_re.astype(F32), lam_im.astype(F32)
    dt = jnp.broadcast_to(jnp.exp(log_dt.astype(F32))[:, None], lr.shape)

    def lam_bar_pow(lr, li, dt, n):
        mag = jnp.exp(lr * dt * n)
        return mag * jnp.cos(li * dt * n), mag * jnp.sin(li * dt * n)

    br_, bi_ = lam_bar_pow(lr, li, dt, 1.0)
    den = lr * lr + li * li
    cr = ((br_ - 1.0) * lr + bi_ * li) / den
    ci = (bi_ * lr - (br_ - 1.0) * li) / den
    bre, bim = b_re.astype(F32), b_im.astype(F32)
    bbar_re = cr[..., None] * bre - ci[..., None] * bim
    bbar_im = cr[..., None] * bim + ci[..., None] * bre
    gs = g // S5_SPLIT
    eye = jnp.eye(gs, dtype=F32)

    def bdiag_in(m):
        return jnp.einsum('gph,gk->ghkp', m, eye).reshape(gs * h16, gs * p)

    def bdiag_out(m):
        return jnp.einsum('ghp,gk->gpkh', m, eye).reshape(gs * p, gs * h16)

    halves = lambda m: [m[hh * gs:(hh + 1) * gs] for hh in range(S5_SPLIT)]
    bmat = jnp.stack([jnp.concatenate([bdiag_in(r), bdiag_in(i)], axis=1)
                      for r, i in zip(halves(bbar_re), halves(bbar_im))])
    cmat = jnp.stack([jnp.concatenate([bdiag_out(r), bdiag_out(i)], axis=0)
                      for r, i in zip(halves(c_re.astype(F32)), halves(-c_im.astype(F32)))])

    row = jnp.arange(SUBLANES, dtype=F32)[:, None]
    planes = []
    for dist in (1, 2, 4):
        pr, pi_ = lam_bar_pow(lr, li, dt, float(dist))
        keep = row >= dist
        planes += [jnp.where(keep, pr.reshape(1, g * p), 0.0), jnp.where(keep, pi_.reshape(1, g * p), 0.0)]
    flat = lambda m: m.reshape(1, g * p)
    pr, pi_ = lam_bar_pow(flat(lr), flat(li), flat(dt), row + 1.0)
    planes += [pr, pi_]
    return bmat.astype(BF16), cmat.astype(BF16), jnp.stack(planes).astype(F32)


def _s5_mixer(proj, bsz, seq, consts, d_skip, w_glu, b_glu, lc=256):
    bmat, cmat, sc = consts
    w = bmat.shape[0] * bmat.shape[1]
    ns = bmat.shape[0] * bmat.shape[2] // 2
    lc = min(lc, seq)
    nc = seq // lc
    const2 = lambda b, c: (0, 0)
    return pl.pallas_call(
        functools.partial(_s5_kernel, ns=ns),
        grid=(bsz, nc),
        in_specs=[
            pl.BlockSpec((lc, w), lambda b, c: (b * nc + c, 0)),
            pl.BlockSpec(bmat.shape, lambda b, c: (0, 0, 0)),
            pl.BlockSpec(cmat.shape, lambda b, c: (0, 0, 0)),
            pl.BlockSpec(sc.shape, lambda b, c: (0, 0, 0)),
            pl.BlockSpec((1, w), const2),
            pl.BlockSpec((w, w), const2),
            pl.BlockSpec((1, w), const2),
        ],
        out_specs=pl.BlockSpec((lc, w), lambda b, c: (b * nc + c, 0)),
        out_shape=jax.ShapeDtypeStruct((bsz * seq, w), F32),
        scratch_shapes=[pltpu.VMEM((lc, 2 * ns), F32), pltpu.VMEM((1, 2 * ns), F32)],
        compiler_params=_params("arbitrary", "arbitrary"),
        name="s5_mixer",
    )(proj, bmat, cmat, sc, d_skip.reshape(1, w), w_glu.astype(BF16), b_glu.reshape(1, w))


def _rope_kernel(pos_ref, freq_ref, cos_ref, sin_ref):
    ang = pos_ref[...].astype(F32) * freq_ref[...]
    cos_ref[...] = jnp.cos(ang)
    sin_ref[...] = jnp.sin(ang)


def _rope_tables(positions, half, tm=2048):
    t = positions.size
    tm = min(tm, t)
    reps = LANES // half
    inv_freq = ROPE_BASE ** (-jnp.arange(half, dtype=F32) / half)
    freq = jnp.tile(inv_freq, reps).reshape(1, LANES)
    return pl.pallas_call(
        _rope_kernel,
        grid=(t // tm,),
        in_specs=[pl.BlockSpec((tm, 1), lambda i: (i, 0)), pl.BlockSpec((1, LANES), lambda i: (0, 0))],
        out_specs=[pl.BlockSpec((tm, LANES), lambda i: (i, 0))] * 2,
        out_shape=[jax.ShapeDtypeStruct((t, LANES), F32)] * 2,
        compiler_params=_params("parallel"),
        name="rope_tables",
    )(positions.reshape(t, 1), freq)


def _ret_kernel(q_ref, k_ref, v_ref, g_ref, cos_ref, sin_ref, dec_ref, zeta_ref, xi_ref, cd_ref,
                gnw_ref, o_ref, state_ref, *, dk, dv):
    @pl.when(pl.program_id(1) == 0)
    def _():
        state_ref[...] = jnp.zeros_like(state_ref)

    lt = q_ref.shape[0]
    half = dk // 2
    qkw = RET_HEADS * dk
    cos = jnp.concatenate([cos_ref[...]] * (qkw // LANES), axis=1)
    sin = jnp.concatenate([sin_ref[...]] * (qkw // LANES), axis=1)
    lane = lax.broadcasted_iota(jnp.int32, (lt, qkw), 1)
    first = (lane % dk) < half
    sin_signed = jnp.where(first, -sin, sin)

    def rot(x):
        swapped = jnp.where(first, pltpu.roll(x, qkw - half, 1), pltpu.roll(x, half, 1))
        return x * cos + swapped * sin_signed

    q = rot(q_ref[...])
    k = rot(k_ref[...]) * (dk ** -0.5)
    v = v_ref[...]
    g = g_ref[...]
    gnw = gnw_ref[...]

    for c0 in range(0, lt, RET_CHUNK):
        for h in range(RET_HEADS):
            qh = q[c0:c0 + RET_CHUNK, h * dk:(h + 1) * dk]
            kh = k[c0:c0 + RET_CHUNK, h * dk:(h + 1) * dk]
            vh = v[c0:c0 + RET_CHUNK, h * dv:(h + 1) * dv]
            st = state_ref[h]
            scores = lax.dot_general(qh, kh, (((1,), (1,)), ((), ())),
                                     preferred_element_type=F32) * dec_ref[h]
            o = jnp.dot(scores, vh, preferred_element_type=F32)
            o = o + jnp.dot(qh, st, preferred_element_type=F32) * xi_ref[h]
            kz = kh * zeta_ref[h]
            kv = lax.dot_general(kz, vh, (((0,), (0,)), ((), ())), preferred_element_type=F32)
            state_ref[h] = cd_ref[h] * st + kv
            mu = jnp.mean(o, axis=-1, keepdims=True)
            oc = o - mu
            var = jnp.mean(oc * oc, axis=-1, keepdims=True)
            on = oc * lax.rsqrt(var + EPS) * gnw[:, h * dv:(h + 1) * dv]
            gh = g[c0:c0 + RET_CHUNK, h * dv:(h + 1) * dv]
            o_ref[c0:c0 + RET_CHUNK, h * dv:(h + 1) * dv] = gh * _sigmoid(gh) * on


def _ret_consts(dk, dv):
    log_gamma = jnp.log1p(-(2.0 ** (-5.0 - jnp.arange(RET_HEADS, dtype=F32))))
    idx = jnp.arange(RET_CHUNK, dtype=F32)
    diff = idx[:, None] - idx[None, :]
    decay = jnp.where(diff >= 0, jnp.exp(log_gamma[:, None, None] * jnp.maximum(diff, 0.0)), 0.0)
    zeta = jnp.exp(log_gamma[:, None] * (RET_CHUNK - 1 - idx))[:, :, None]
    xi = jnp.exp(log_gamma[:, None] * (idx + 1.0))[:, :, None]
    cd = jnp.broadcast_to(jnp.exp(log_gamma * RET_CHUNK)[:, None, None], (RET_HEADS, dk, dv))
    return decay, jnp.broadcast_to(zeta, (RET_HEADS, RET_CHUNK, dk)), \
        jnp.broadcast_to(xi, (RET_HEADS, RET_CHUNK, dv)), cd


def _ret_mixer(proj, bsz, seq, cos_t, sin_t, gn_w, qk_w, v_w, q_col, lt=512):
    dk = qk_w // RET_HEADS
    dv = v_w // RET_HEADS
    lt = min(lt, seq)
    nc = seq // lt
    decay, zeta, xi, cd = _ret_consts(dk, dv)
    qb = q_col // qk_w
    vb = (q_col + 2 * qk_w) // v_w
    row = lambda b, c: b * nc + c
    c3 = lambda b, c: (0, 0, 0)
    return pl.pallas_call(
        functools.partial(_ret_kernel, dk=dk, dv=dv),
        grid=(bsz, nc),
        in_specs=[
            pl.BlockSpec((lt, qk_w), lambda b, c: (row(b, c), qb)),
            pl.BlockSpec((lt, qk_w), lambda b, c: (row(b, c), qb + 1)),
            pl.BlockSpec((lt, v_w), lambda b, c: (row(b, c), vb)),
            pl.BlockSpec((lt, v_w), lambda b, c: (row(b, c), vb + 1)),
            pl.BlockSpec((lt, LANES), lambda b, c: (row(b, c), 0)),
            pl.BlockSpec((lt, LANES), lambda b, c: (row(b, c), 0)),
            pl.BlockSpec(decay.shape, c3),
            pl.BlockSpec(zeta.shape, c3),
            pl.BlockSpec(xi.shape, c3),
            pl.BlockSpec(cd.shape, c3),
            pl.BlockSpec((1, v_w), lambda b, c: (0, 0)),
        ],
        out_specs=pl.BlockSpec((lt, v_w), lambda b, c: (row(b, c), 0)),
        out_shape=jax.ShapeDtypeStruct((bsz * seq, v_w), F32),
        scratch_shapes=[pltpu.VMEM((RET_HEADS, dk, dv), F32)],
        compiler_params=_params("arbitrary", "arbitrary"),
        name="retention_mixer",
    )(proj, proj, proj, proj, cos_t, sin_t, decay, zeta, xi, cd, gn_w.reshape(1, v_w))


def _lru_kernel(x_ref, gate_ref, cw_ref, cb_ref, wri_ref, bri_ref, sp_ref, o_ref,
                xe_ref, a_ref, b_ref, carry_ref):
    lc, w = x_ref.shape

    @pl.when(pl.program_id(1) == 0)
    def _():
        xe_ref[0:SUBLANES, :] = jnp.zeros((SUBLANES, w), F32)
        carry_ref[...] = jnp.zeros_like(carry_ref)

    x = x_ref[...]
    xe_ref[SUBLANES:, :] = x
    xc = cw_ref[CONV_WIDTH - 1:CONV_WIDTH, :] * x + cb_ref[...]
    for j in range(1, CONV_WIDTH):
        xc = xc + cw_ref[CONV_WIDTH - 1 - j:CONV_WIDTH - j, :] * xe_ref[pl.ds(SUBLANES - j, lc), :]
    xe_ref[0:SUBLANES, :] = x[lc - SUBLANES:, :]

    ri = _sigmoid(jnp.dot(xc.astype(BF16), wri_ref[...], preferred_element_type=F32) + bri_ref[...])
    r = ri[:, :w]
    gi = ri[:, w:]
    log_a = -LRU_C * r * sp_ref[...]
    a = jnp.exp(log_a)
    mult = jnp.sqrt(1.0 - jnp.exp(2.0 * log_a))
    b = mult * (gi * xc)

    rowmod = lax.broadcasted_iota(jnp.int32, (lc, w), 0) % SUBLANES
    for dist in (1, 2, 4):
        keep = rowmod >= dist
        a_sh = jnp.where(keep, pltpu.roll(a, dist, 0), 1.0)
        b_sh = jnp.where(keep, pltpu.roll(b, dist, 0), 0.0)
        b = a * b_sh + b
        a = a * a_sh
    a_ref[...] = a
    b_ref[...] = b

    def body(r8, hprev):
        rows = pl.ds(pl.multiple_of(r8 * SUBLANES, SUBLANES), SUBLANES)
        h = a_ref[rows, :] * hprev + b_ref[rows, :]
        b_ref[rows, :] = h
        return h[SUBLANES - 1:SUBLANES, :]

    carry_ref[...] = lax.fori_loop(0, lc // SUBLANES, body, carry_ref[...])
    o_ref[...] = b_ref[...] * _gelu(gate_ref[...])


def _lru_mixer(proj, bsz, seq, conv_w, conv_b, w_r, b_r, w_i, b_i, lam, w, x_col, lc=512):
    lc = min(lc, seq)
    nc = seq // lc
    eye = jnp.eye(LRU_BLOCKS, dtype=F32)
    dense = lambda m: jnp.einsum('ncd,nm->ncmd', m, eye).reshape(w, w)
    wri = jnp.concatenate([dense(w_r), dense(w_i)], axis=1).astype(BF16)
    bri = jnp.concatenate([b_r, b_i]).reshape(1, 2 * w)
    sp = jax.nn.softplus(-lam.astype(F32)).reshape(1, w)
    xb = x_col // w
    c2 = lambda b, c: (0, 0)
    return pl.pallas_call(
        _lru_kernel,
        grid=(bsz, nc),
        in_specs=[
            pl.BlockSpec((lc, w), lambda b, c: (b * nc + c, xb)),
            pl.BlockSpec((lc, w), lambda b, c: (b * nc + c, xb + 1)),
            pl.BlockSpec((CONV_WIDTH, w), c2),
            pl.BlockSpec((1, w), c2),
            pl.BlockSpec((w, 2 * w), c2),
            pl.BlockSpec((1, 2 * w), c2),
            pl.BlockSpec((1, w), c2),
        ],
        out_specs=pl.BlockSpec((lc, w), lambda b, c: (b * nc + c, 0)),
        out_shape=jax.ShapeDtypeStruct((bsz * seq, w), F32),
        scratch_shapes=[pltpu.VMEM((lc + SUBLANES, w), F32), pltpu.VMEM((lc, w), F32),
                        pltpu.VMEM((lc, w), F32), pltpu.VMEM((1, w), F32)],
        compiler_params=_params("arbitrary", "arbitrary"),
        name="rglru_mixer",
    )(proj, proj, conv_w, conv_b.reshape(1, w), wri, bri, sp)


def _merge_kernel(y0_ref, y1_ref, y2_ref, gl_ref, x_ref, wb_ref, wo_ref, o_ref):
    d = x_ref.shape[1]
    merged = None
    for n, y_ref in enumerate((y0_ref, y1_ref, y2_ref)):
        br = jnp.dot(y_ref[...].astype(BF16), wb_ref[n], preferred_element_type=F32)
        term = _sigmoid(gl_ref[:, n * d:(n + 1) * d]) * br
        merged = term if merged is None else merged + term
    o_ref[...] = x_ref[...] + jnp.dot(merged.astype(BF16), wo_ref[...], preferred_element_type=F32)


def _merge(ys, proj, x2, w_branch, w_out, gate_col, tm=512):
    t, d = x2.shape
    bw = ys[0].shape[1]
    tm = min(tm, t)
    gb = gate_col // (N_BRANCH * d)
    yspec = pl.BlockSpec((tm, bw), lambda i: (i, 0))
    return pl.pallas_call(
        _merge_kernel,
        grid=(t // tm,),
        in_specs=[yspec, yspec, yspec,
                  pl.BlockSpec((tm, N_BRANCH * d), lambda i: (i, gb)),
                  pl.BlockSpec((tm, d), lambda i: (i, 0)),
                  pl.BlockSpec((N_BRANCH, bw, d), lambda i: (0, 0, 0)),
                  pl.BlockSpec((d, d), lambda i: (0, 0))],
        out_specs=pl.BlockSpec((tm, d), lambda i: (i, 0)),
        out_shape=jax.ShapeDtypeStruct((t, d), F32),
        compiler_params=_params("parallel"),
        name="merge_out",
    )(ys[0], ys[1], ys[2], proj, x2, w_branch.astype(BF16), w_out.astype(BF16))


_NEG = -1e30
PEER_SUB = 512
BF16_ROWS = 16
PEER_FLAGS = None


def _top_rows(s, count, n_ranked=0):
    rows = []
    rank = None
    if n_ranked:
        rank = jnp.full(s.shape, float(n_ranked), F32)
    for i in range(count):
        m = jnp.max(s, axis=0, keepdims=True)
        rows.append(m)
        if i + 1 < count or i < n_ranked:
            hit = s >= m
            if i < n_ranked:
                rank = jnp.where(hit, float(i), rank)
            s = jnp.where(hit, _NEG, s)
    return (rows, rank) if n_ranked else rows


def _batcher_pairs(lo, hi):
    def merge(lo, hi, r):
        step = r * 2
        if step < hi - lo:
            yield from merge(lo, hi, step)
            yield from merge(lo + r, hi, step)
            for i in range(lo + r, hi - r, step):
                yield (i, i + r)
        else:
            yield (lo, lo + r)

    if hi - lo >= 1:
        mid = lo + (hi - lo) // 2
        yield from _batcher_pairs(lo, mid)
        yield from _batcher_pairs(mid + 1, hi)
        yield from merge(lo, hi, 1)


def _top16_chunks(s):
    nrow = s.shape[0] // SUBLANES
    chunks = [s[r * SUBLANES:(r + 1) * SUBLANES, :] for r in range(nrow)]
    c = list(chunks)
    for i, j in _batcher_pairs(0, nrow - 1):
        c[i], c[j] = jnp.maximum(c[i], c[j]), jnp.minimum(c[i], c[j])
    for shift in (4, 2, 1):
        c = [jnp.maximum(c[r], pltpu.roll(c[nrow - 1 - r], shift, 0)) for r in range(nrow)]
        dist = nrow // 2
        while dist >= 1:
            for i in range(nrow):
                if not i & dist:
                    c[i], c[i + dist] = jnp.maximum(c[i], c[i + dist]), jnp.minimum(c[i], c[i + dist])
            dist //= 2
    rest = None
    for ch in chunks:
        m = jnp.where(ch < c[nrow - 1], ch, _NEG)
        rest = m if rest is None else jnp.maximum(rest, m)
    for shift in (4, 2, 1):
        rest = jnp.maximum(rest, pltpu.roll(rest, shift, 0))
    return chunks, c, rest


def _gelu_gate(x):
    k1 = -2.0 * _SQRT_2_OVER_PI
    k3 = k1 * 0.044715
    return x / (1.0 + jnp.exp(x * (k1 + k3 * (x * x))))


def _peer_kernel(x_ref, nw_ref, wqt_ref, keys_ref, u_ref, vt_ref, fw_ref, o_ref,
                 xn_ref, qt_ref, rank_ref, e2_ref, cnt_ref, g_ref, acc_ref, *, final_norm):
    e = pl.program_id(1)
    ne = pl.num_programs(1)
    eb = u_ref.shape[0]
    tm = x_ref.shape[0]
    nk = PEER_NKEYS
    kk = PEER_TOPK + 1
    ntile = nk // BF16_ROWS

    @pl.when(e == 0)
    def _prep():
        xn = _rms(x_ref[...], nw_ref[...]).astype(BF16)
        xn_ref[...] = xn
        qt_ref[...] = lax.dot_general(wqt_ref[...], xn, (((1,), (1,)), ((), ())),
                                      preferred_element_type=F32)
        acc_ref[...] = jnp.zeros_like(acc_ref)

        def head(h, _):
            q1 = qt_ref[pl.ds(pl.multiple_of(h * 2 * PEER_HALF, PEER_HALF), PEER_HALF), :]
            q2 = qt_ref[pl.ds(pl.multiple_of(h * 2 * PEER_HALF + PEER_HALF, PEER_HALF), PEER_HALF), :]
            s1 = jnp.dot(keys_ref[2 * h], q1, preferred_element_type=F32)
            s2 = jnp.dot(keys_ref[2 * h + 1], q2, preferred_element_type=F32)
            ch1, t1, n1 = _top16_chunks(s1)
            ch2, t2, n2 = _top16_chunks(s2)
            v1 = [t[0:1, :] for t in t1] + [n1[0:1, :]]
            v2 = [t[0:1, :] for t in t2] + [n2[0:1, :]]
            cands = [v1[i] + v2[j] for i in range(kk) for j in range(kk) if (i + 1) * (j + 1) <= kk]
            cand = jnp.concatenate(cands, axis=0)
            top = _top_rows(cand, kk)
            theta = 0.5 * (top[PEER_TOPK - 1] + top[PEER_TOPK])
            cmax = v1[0] + v2[0]
            z = jnp.sum(jnp.where(cand >= theta, jnp.exp(cand - cmax), 0.0), axis=0, keepdims=True)
            theta8 = jnp.broadcast_to(theta, (SUBLANES, tm))
            zinv8 = jnp.broadcast_to(1.0 / z, (SUBLANES, tm))
            ranks, e2s = [], []
            for r, (c1, c2) in enumerate(zip(ch1, ch2)):
                rows = slice(r * SUBLANES, (r + 1) * SUBLANES)
                thr = theta8 - c1
                cnt = jnp.zeros_like(c1)
                rank = jnp.full(c2.shape, float(PEER_TOPK), F32)
                for j in range(PEER_TOPK):
                    cnt = jnp.where(t2[j] >= thr, float(j + 1), cnt)
                    jr = PEER_TOPK - 1 - j
                    rank = jnp.where(c2 >= t2[jr], float(jr), rank)
                cnt_ref[h, rows, :] = cnt
                g_ref[h, rows, :] = jnp.exp(c1 - t1[0]) * zinv8
                ranks.append(rank)
                e2s.append(jnp.exp(c2 - t2[0]))
            per = BF16_ROWS // SUBLANES
            for r in range(ntile):
                rank_ref[h, r] = jnp.concatenate(ranks[r * per:(r + 1) * per], axis=0).astype(BF16)
                e2_ref[h, r] = jnp.concatenate(e2s[r * per:(r + 1) * per], axis=0).astype(BF16)
            return 0

        lax.fori_loop(0, PEER_HEADS, head, 0)

    nsub = eb // PEER_SUB
    a_per = PEER_SUB // nk

    def gate(j):
        cnts, grs = {}, {}
        for ai in range(a_per):
            a = e * (eb // nk) + j * a_per + ai
            for h in range(PEER_HEADS):
                cnts[ai, h] = jnp.broadcast_to(cnt_ref[h, pl.ds(a, 1), :], (BF16_ROWS, tm)).astype(BF16)
                grs[ai, h] = jnp.broadcast_to(g_ref[h, pl.ds(a, 1), :], (BF16_ROWS, tm)).astype(BF16)
        wgts = [[None] * ntile for _ in range(a_per)]
        for r in range(ntile):
            for h in range(PEER_HEADS):
                rk = rank_ref[h, r]
                ee = e2_ref[h, r]
                for ai in range(a_per):
                    term = jnp.where(rk < cnts[ai, h], ee * grs[ai, h], 0.0)
                    wgts[ai][r] = term if h == 0 else wgts[ai][r] + term
        return wgts

    def score(j):
        return lax.dot_general(u_ref[j * PEER_SUB:(j + 1) * PEER_SUB, :], xn_ref[...],
                               (((1,), (1,)), ((), ())), preferred_element_type=F32)

    def hidden(act, wgts):
        parts = []
        for ai in range(a_per):
            gel = _gelu_gate(act[ai * nk:(ai + 1) * nk, :]).astype(BF16)
            for r in range(ntile):
                parts.append(gel[r * BF16_ROWS:(r + 1) * BF16_ROWS, :] * wgts[ai][r])
        return jnp.concatenate(parts, axis=0)

    hid = hidden(score(0), gate(0))
    for j in range(nsub):
        if j + 1 < nsub:
            wgts = gate(j + 1)
            act = score(j + 1)
        acc_ref[...] += jnp.dot(vt_ref[j], hid, preferred_element_type=F32)
        if j + 1 < nsub:
            hid = hidden(act, wgts)

    @pl.when(e == ne - 1)
    def _fin():
        out = x_ref[...] + acc_ref[...].T
        if final_norm:
            out = _rms(out, fw_ref[...])
        o_ref[...] = out


def _peer_values(v):
    n_exp, d = v.shape
    return v.astype(BF16).reshape(n_exp // PEER_SUB, PEER_SUB, d).transpose(0, 2, 1)


def _peer(x2, norm_w, w_query, sub_keys, u_bf16, vt_bf16, final_w, final_norm, tm=512, eb=2048):
    t, d = x2.shape
    n_exp = u_bf16.shape[0]
    tm = min(tm, t)
    wqt = w_query.T.astype(BF16)
    keys = sub_keys.reshape(PEER_HEADS * 2, PEER_NKEYS, PEER_HALF).astype(F32)
    c2 = lambda i, e: (0, 0)
    head_buf = pltpu.VMEM((PEER_HEADS, PEER_NKEYS, tm), F32)
    packed_buf = pltpu.VMEM((PEER_HEADS, PEER_NKEYS // BF16_ROWS, BF16_ROWS, tm), BF16)
    return pl.pallas_call(
        functools.partial(_peer_kernel, final_norm=final_norm),
        grid=(t // tm, n_exp // eb),
        in_specs=[
            pl.BlockSpec((tm, d), lambda i, e: (i, 0)),
            pl.BlockSpec((1, d), c2),
            pl.BlockSpec(wqt.shape, c2),
            pl.BlockSpec(keys.shape, lambda i, e: (0, 0, 0)),
            pl.BlockSpec((eb, d), lambda i, e: (e, 0)),
            pl.BlockSpec((eb // PEER_SUB, d, PEER_SUB), lambda i, e: (e, 0, 0)),
            pl.BlockSpec((1, d), c2),
        ],
        out_specs=pl.BlockSpec((tm, d), lambda i, e: (i, 0)),
        out_shape=jax.ShapeDtypeStruct((t, d), F32),
        scratch_shapes=[pltpu.VMEM((tm, d), BF16), pltpu.VMEM((wqt.shape[0], tm), F32),
                        packed_buf, packed_buf, head_buf, head_buf, pltpu.VMEM((d, tm), F32)],
        compiler_params=_params("parallel", "arbitrary", flags=PEER_FLAGS),
        name="peer_dense",
    )(x2, norm_w.reshape(1, d), wqt, keys, u_bf16, vt_bf16, final_w.reshape(1, d))


def kernel(x, positions, norm_mix, w_in, s5_lam_re, s5_lam_im, s5_log_dt, s5_b_re, s5_b_im, s5_c_re, s5_c_im, s5_d, s5_w_glu, s5_b_glu, ret_gn_w, lru_conv_w, lru_conv_b, lru_w_r, lru_b_r, lru_w_i, lru_b_i, lru_lam, w_branch, w_out, norm_ffn, peer_w_query, peer_sub_keys, peer_u, peer_v, final_norm):
    bsz, seq, d = x.shape
    depth = norm_mix.shape[0]
    s5_w = s5_d.shape[1]
    v_w = ret_gn_w.shape[1]
    qk_w = v_w // 2
    lru_w = lru_lam.shape[1]
    q_col = s5_w
    x_col = q_col + 2 * qk_w + 2 * v_w
    gate_col = x_col + 2 * lru_w

    x2 = x.reshape(bsz * seq, d)
    cos_t, sin_t = _rope_tables(positions, qk_w // RET_HEADS // 2)
    for l in range(depth):
        proj = _norm_proj(x2, norm_mix[l], w_in[l].astype(BF16))
        s5c = _s5_consts(s5_lam_re[l], s5_lam_im[l], s5_log_dt[l], s5_b_re[l], s5_b_im[l],
                         s5_c_re[l], s5_c_im[l])
        y_s5 = _s5_mixer(proj, bsz, seq, s5c, s5_d[l], s5_w_glu[l], s5_b_glu[l])
        y_ret = _ret_mixer(proj, bsz, seq, cos_t, sin_t, ret_gn_w[l], qk_w, v_w, q_col)
        y_lru = _lru_mixer(proj, bsz, seq, lru_conv_w[l], lru_conv_b[l], lru_w_r[l], lru_b_r[l],
                           lru_w_i[l], lru_b_i[l], lru_lam[l], lru_w, x_col)
        x2 = _merge((y_s5, y_ret, y_lru), proj, x2, w_branch[l], w_out[l], gate_col)
        x2 = _peer(x2, norm_ffn[l], peer_w_query[l], peer_sub_keys[l], peer_u[l].astype(BF16),
                   _peer_values(peer_v[l]), final_norm, final_norm=(l == depth - 1))
    return x2.reshape(bsz, seq, d)
```

```python
import functools
import math

import jax
import jax.numpy as jnp
import numpy as np
from jax import lax
from jax.experimental import pallas as pl
from jax.experimental.pallas import tpu as pltpu

F32 = jnp.float32
BF16 = jnp.bfloat16
ACT_DTYPE = BF16

EPS = 1e-6
S5_GROUP = 16
S5_STATE = 64
RET_HEADS = 4
RET_CHUNK = 128
ROPE_BASE = 10000.0
LRU_BLOCKS = 8
CONV_WIDTH = 4
LRU_C = 8.0
N_BRANCH = 3
PEER_HEADS = 8
PEER_NKEYS = 128
PEER_TOPK = 16
PEER_HALF = 64

SUBLANES = 8
LANES = 128
VMEM_LIMIT = 56 * 1024 * 1024

_SQRT_2_OVER_PI = math.sqrt(2.0 / math.pi)


def _gelu(x):
    return 0.5 * x * (1.0 + jnp.tanh(_SQRT_2_OVER_PI * (x + 0.044715 * (x * x * x))))


def _sigmoid(x):
    return 1.0 / (1.0 + jnp.exp(-x))


def _rms(x, w):
    return x * lax.rsqrt(jnp.mean(x * x, axis=-1, keepdims=True) + EPS) * w


def _params(*sem, flags=None):
    return pltpu.CompilerParams(dimension_semantics=sem, vmem_limit_bytes=VMEM_LIMIT, flags=flags)


def _norm_proj_kernel(x_ref, nw_ref, w_ref, o_ref, xn_ref):
    @pl.when(pl.program_id(1) == 0)
    def _():
        xn_ref[...] = _rms(x_ref[...], nw_ref[...]).astype(BF16)

    o_ref[...] = jnp.dot(xn_ref[...], w_ref[...], preferred_element_type=F32).astype(o_ref.dtype)


def _norm_proj(x2, norm_w, w_bf16, tm=1024, tn=1024):
    t, d = x2.shape
    n = w_bf16.shape[1]
    tm = min(tm, t)
    return pl.pallas_call(
        _norm_proj_kernel,
        grid=(t // tm, n // tn),
        in_specs=[
            pl.BlockSpec((tm, d), lambda i, j: (i, 0)),
            pl.BlockSpec((1, d), lambda i, j: (0, 0)),
            pl.BlockSpec((d, tn), lambda i, j: (0, j)),
        ],
        out_specs=pl.BlockSpec((tm, tn), lambda i, j: (i, j)),
        out_shape=jax.ShapeDtypeStruct((t, n), ACT_DTYPE),
        scratch_shapes=[pltpu.VMEM((tm, d), BF16)],
        compiler_params=_params("parallel", "arbitrary"),
        name="norm_proj",
    )(x2, norm_w.reshape(1, d), w_bf16)


S5_LANE_CHUNK = 512
S5_SPLIT = 2


def _s5_kernel(u_ref, bmat_ref, cmat_ref, sc_ref, d_ref, wglu_ref, bglu_ref, o_ref,
               bu_ref, carry_ref, *, ns):
    nh = ns // S5_SPLIT

    @pl.when(pl.program_id(1) == 0)
    def _():
        carry_ref[...] = jnp.zeros_like(carry_ref)

    u = u_ref[...].astype(F32)
    ub = u.astype(BF16)
    lc, w = u.shape
    for hh in range(S5_SPLIT):
        part = jnp.dot(ub[:, hh * (w // S5_SPLIT):(hh + 1) * (w // S5_SPLIT)], bmat_ref[hh],
                       preferred_element_type=F32)
        bu_ref[:, hh * nh:(hh + 1) * nh] = part[:, :nh]
        bu_ref[:, ns + hh * nh:ns + (hh + 1) * nh] = part[:, nh:]

    for c0 in range(0, ns, S5_LANE_CHUNK):
        re_sl = pl.ds(c0, S5_LANE_CHUNK)
        im_sl = pl.ds(ns + c0, S5_LANE_CHUNK)
        consts = [sc_ref[k, :, c0:c0 + S5_LANE_CHUNK] for k in range(8)]

        def body(r, carry, re_sl=re_sl, im_sl=im_sl, consts=consts):
            hpr, hpi = carry
            rows = pl.ds(pl.multiple_of(r * SUBLANES, SUBLANES), SUBLANES)
            br = bu_ref[rows, re_sl]
            bi = bu_ref[rows, im_sl]
            for lvl, dist in enumerate((1, 2, 4)):
                ar, ai = consts[2 * lvl], consts[2 * lvl + 1]
                sr = pltpu.roll(br, dist, 0)
                si = pltpu.roll(bi, dist, 0)
                br, bi = br + ar * sr - ai * si, bi + ar * si + ai * sr
            pr, pim = consts[6], consts[7]
            hr = br + pr * hpr - pim * hpi
            hi = bi + pr * hpi + pim * hpr
            bu_ref[rows, re_sl] = hr
            bu_ref[rows, im_sl] = hi
            return hr[SUBLANES - 1:SUBLANES, :], hi[SUBLANES - 1:SUBLANES, :]

        hr_last, hi_last = lax.fori_loop(
            0, lc // SUBLANES, body, (carry_ref[:, re_sl], carry_ref[:, im_sl]))
        carry_ref[:, re_sl] = hr_last
        carry_ref[:, im_sl] = hi_last

    ys = []
    for hh in range(S5_SPLIT):
        hs = jnp.concatenate([bu_ref[:, hh * nh:(hh + 1) * nh],
                              bu_ref[:, ns + hh * nh:ns + (hh + 1) * nh]], axis=1)
        ys.append(jnp.dot(hs.astype(BF16), cmat_ref[hh], preferred_element_type=F32))
    y = jnp.concatenate(ys, axis=1)
    z = _gelu(y + d_ref[...] * u)
    gl = jnp.dot(z.astype(BF16), wglu_ref[...], preferred_element_type=F32) + bglu_ref[...]
    o_ref[...] = (z * _sigmoid(gl)).astype(o_ref.dtype)


def _s5_consts(lam_re, lam_im, log_dt, b_re, b_im, c_re, c_im):
    g, p = lam_re.shape
    h16 = b_re.shape[-1]
    lr, li = lam_re.astype(F32), lam_im.astype(F32)
    dt = jnp.broadcast_to(jnp.exp(log_dt.astype(F32))[:, None], lr.shape)

    def lam_bar_pow(lr, li, dt, n):
        mag = jnp.exp(lr * dt * n)
        return mag * jnp.cos(li * dt * n), mag * jnp.sin(li * dt * n)

    br_, bi_ = lam_bar_pow(lr, li, dt, 1.0)
    den = lr * lr + li * li
    cr = ((br_ - 1.0) * lr + bi_ * li) / den
    ci = (bi_ * lr - (br_ - 1.0) * li) / den
    bre, bim = b_re.astype(F32), b_im.astype(F32)
    bbar_re = cr[..., None] * bre - ci[..., None] * bim
    bbar_im = cr[..., None] * bim + ci[..., None] * bre
    gs = g // S5_SPLIT
    eye = jnp.eye(gs, dtype=F32)

    def bdiag_in(m):
        return jnp.einsum('gph,gk->ghkp', m, eye).reshape(gs * h16, gs * p)

    def bdiag_out(m):
        return jnp.einsum('ghp,gk->gpkh', m, eye).reshape(gs * p, gs * h16)

    halves = lambda m: [m[hh * gs:(hh + 1) * gs] for hh in range(S5_SPLIT)]
    bmat = jnp.stack([jnp.concatenate([bdiag_in(r), bdiag_in(i)], axis=1)
                      for r, i in zip(halves(bbar_re), halves(bbar_im))])
    cmat = jnp.stack([jnp.concatenate([bdiag_out(r), bdiag_out(i)], axis=0)
                      for r, i in zip(halves(c_re.astype(F32)), halves(-c_im.astype(F32)))])

    row = jnp.arange(SUBLANES, dtype=F32)[:, None]
    planes = []
    for dist in (1, 2, 4):
        pr, pi_ = lam_bar_pow(lr, li, dt, float(dist))
        keep = row >= dist
        planes += [jnp.where(keep, pr.reshape(1, g * p), 0.0), jnp.where(keep, pi_.reshape(1, g * p), 0.0)]
    flat = lambda m: m.reshape(1, g * p)
    pr, pi_ = lam_bar_pow(flat(lr), flat(li), flat(dt), row + 1.0)
    planes += [pr, pi_]
    return bmat.astype(BF16), cmat.astype(BF16), jnp.stack(planes).astype(F32)


def _s5_mixer(proj, bsz, seq, consts, d_skip, w_glu, b_glu, lc=256):
    bmat, cmat, sc = consts
    w = bmat.shape[0] * bmat.shape[1]
    ns = bmat.shape[0] * bmat.shape[2] // 2
    lc = min(lc, seq)
    nc = seq // lc
    const2 = lambda b, c: (0, 0)
    return pl.pallas_call(
        functools.partial(_s5_kernel, ns=ns),
        grid=(bsz, nc),
        in_specs=[
            pl.BlockSpec((lc, w), lambda b, c: (b * nc + c, 0)),
            pl.BlockSpec(bmat.shape, lambda b, c: (0, 0, 0)),
            pl.BlockSpec(cmat.shape, lambda b, c: (0, 0, 0)),
            pl.BlockSpec(sc.shape, lambda b, c: (0, 0, 0)),
            pl.BlockSpec((1, w), const2),
            pl.BlockSpec((w, w), const2),
            pl.BlockSpec((1, w), const2),
        ],
        out_specs=pl.BlockSpec((lc, w), lambda b, c: (b * nc + c, 0)),
        out_shape=jax.ShapeDtypeStruct((bsz * seq, w), ACT_DTYPE),
        scratch_shapes=[pltpu.VMEM((lc, 2 * ns), F32), pltpu.VMEM((1, 2 * ns), F32)],
        compiler_params=_params("arbitrary", "arbitrary"),
        name="s5_mixer",
    )(proj, bmat, cmat, sc, d_skip.reshape(1, w), w_glu.astype(BF16), b_glu.reshape(1, w))


def _rope_kernel(pos_ref, freq_ref, cos_ref, sin_ref):
    ang = pos_ref[...].astype(F32) * freq_ref[...]
    cos_ref[...] = jnp.cos(ang)
    sin_ref[...] = jnp.sin(ang)


def _rope_tables(positions, half, tm=2048):
    t = positions.size
    tm = min(tm, t)
    reps = LANES // half
    inv_freq = ROPE_BASE ** (-jnp.arange(half, dtype=F32) / half)
    freq = jnp.tile(inv_freq, reps).reshape(1, LANES)
    return pl.pallas_call(
        _rope_kernel,
        grid=(t // tm,),
        in_specs=[pl.BlockSpec((tm, 1), lambda i: (i, 0)), pl.BlockSpec((1, LANES), lambda i: (0, 0))],
        out_specs=[pl.BlockSpec((tm, LANES), lambda i: (i, 0))] * 2,
        out_shape=[jax.ShapeDtypeStruct((t, LANES), F32)] * 2,
        compiler_params=_params("parallel"),
        name="rope_tables",
    )(positions.reshape(t, 1), freq)


def _ret_kernel(q_ref, k_ref, v_ref, g_ref, cos_ref, sin_ref, dec_ref, zeta_ref, xi_ref, cd_ref,
                gnw_ref, o_ref, state_ref, *, dk, dv):
    @pl.when(pl.program_id(1) == 0)
    def _():
        state_ref[...] = jnp.zeros_like(state_ref)

    lt = q_ref.shape[0]
    half = dk // 2
    qkw = RET_HEADS * dk
    cos = jnp.concatenate([cos_ref[...]] * (qkw // LANES), axis=1)
    sin = jnp.concatenate([sin_ref[...]] * (qkw // LANES), axis=1)
    lane = lax.broadcasted_iota(jnp.int32, (lt, qkw), 1)
    first = (lane % dk) < half
    sin_signed = jnp.where(first, -sin, sin)

    def rot(x):
        swapped = jnp.where(first, pltpu.roll(x, qkw - half, 1), pltpu.roll(x, half, 1))
        return x * cos + swapped * sin_signed

    q = rot(q_ref[...].astype(F32))
    k = rot(k_ref[...].astype(F32)) * (dk ** -0.5)
    v = v_ref[...].astype(F32)
    g = g_ref[...].astype(F32)
    gnw = gnw_ref[...]

    for c0 in range(0, lt, RET_CHUNK):
        for h in range(RET_HEADS):
            qh = q[c0:c0 + RET_CHUNK, h * dk:(h + 1) * dk]
            kh = k[c0:c0 + RET_CHUNK, h * dk:(h + 1) * dk]
            vh = v[c0:c0 + RET_CHUNK, h * dv:(h + 1) * dv]
            st = state_ref[h]
            scores = lax.dot_general(qh, kh, (((1,), (1,)), ((), ())),
                                     preferred_element_type=F32) * dec_ref[h]
            o = jnp.dot(scores, vh, preferred_element_type=F32)
            o = o + jnp.dot(qh, st, preferred_element_type=F32) * xi_ref[h]
            kz = kh * zeta_ref[h]
            kv = lax.dot_general(kz, vh, (((0,), (0,)), ((), ())), preferred_element_type=F32)
            state_ref[h] = cd_ref[h] * st + kv
            mu = jnp.mean(o, axis=-1, keepdims=True)
            oc = o - mu
            var = jnp.mean(oc * oc, axis=-1, keepdims=True)
            on = oc * lax.rsqrt(var + EPS) * gnw[:, h * dv:(h + 1) * dv]
            gh = g[c0:c0 + RET_CHUNK, h * dv:(h + 1) * dv]
            o_ref[c0:c0 + RET_CHUNK, h * dv:(h + 1) * dv] = (gh * _sigmoid(gh) * on).astype(o_ref.dtype)


def _ret_consts(dk, dv):
    log_gamma = jnp.log1p(-(2.0 ** (-5.0 - jnp.arange(RET_HEADS, dtype=F32))))
    idx = jnp.arange(RET_CHUNK, dtype=F32)
    diff = idx[:, None] - idx[None, :]
    decay = jnp.where(diff >= 0, jnp.exp(log_gamma[:, None, None] * jnp.maximum(diff, 0.0)), 0.0)
    zeta = jnp.exp(log_gamma[:, None] * (RET_CHUNK - 1 - idx))[:, :, None]
    xi = jnp.exp(log_gamma[:, None] * (idx + 1.0))[:, :, None]
    cd = jnp.broadcast_to(jnp.exp(log_gamma * RET_CHUNK)[:, None, None], (RET_HEADS, dk, dv))
    return decay, jnp.broadcast_to(zeta, (RET_HEADS, RET_CHUNK, dk)), \
        jnp.broadcast_to(xi, (RET_HEADS, RET_CHUNK, dv)), cd


def _ret_mixer(proj, bsz, seq, cos_t, sin_t, gn_w, qk_w, v_w, q_col, lt=512):
    dk = qk_w // RET_HEADS
    dv = v_w // RET_HEADS
    lt = min(lt, seq)
    nc = seq // lt
    decay, zeta, xi, cd = _ret_consts(dk, dv)
    qb = q_col // qk_w
    vb = (q_col + 2 * qk_w) // v_w
    row = lambda b, c: b * nc + c
    c3 = lambda b, c: (0, 0, 0)
    return pl.pallas_call(
        functools.partial(_ret_kernel, dk=dk, dv=dv),
        grid=(bsz, nc),
        in_specs=[
            pl.BlockSpec((lt, qk_w), lambda b, c: (row(b, c), qb)),
            pl.BlockSpec((lt, qk_w), lambda b, c: (row(b, c), qb + 1)),
            pl.BlockSpec((lt, v_w), lambda b, c: (row(b, c), vb)),
            pl.BlockSpec((lt, v_w), lambda b, c: (row(b, c), vb + 1)),
            pl.BlockSpec((lt, LANES), lambda b, c: (row(b, c), 0)),
            pl.BlockSpec((lt, LANES), lambda b, c: (row(b, c), 0)),
            pl.BlockSpec(decay.shape, c3),
            pl.BlockSpec(zeta.shape, c3),
            pl.BlockSpec(xi.shape, c3),
            pl.BlockSpec(cd.shape, c3),
            pl.BlockSpec((1, v_w), lambda b, c: (0, 0)),
        ],
        out_specs=pl.BlockSpec((lt, v_w), lambda b, c: (row(b, c), 0)),
        out_shape=jax.ShapeDtypeStruct((bsz * seq, v_w), ACT_DTYPE),
        scratch_shapes=[pltpu.VMEM((RET_HEADS, dk, dv), F32)],
        compiler_params=_params("arbitrary", "arbitrary"),
        name="retention_mixer",
    )(proj, proj, proj, proj, cos_t, sin_t, decay, zeta, xi, cd, gn_w.reshape(1, v_w))


def _lru_kernel(x_ref, gate_ref, cw_ref, cb_ref, wri_ref, bri_ref, sp_ref, o_ref,
                xe_ref, a_ref, b_ref, carry_ref):
    lc, w = x_ref.shape

    @pl.when(pl.program_id(1) == 0)
    def _():
        xe_ref[0:SUBLANES, :] = jnp.zeros((SUBLANES, w), F32)
        carry_ref[...] = jnp.zeros_like(carry_ref)

    x = x_ref[...].astype(F32)
    xe_ref[SUBLANES:, :] = x
    xc = cw_ref[CONV_WIDTH - 1:CONV_WIDTH, :] * x + cb_ref[...]
    for j in range(1, CONV_WIDTH):
        xc = xc + cw_ref[CONV_WIDTH - 1 - j:CONV_WIDTH - j, :] * xe_ref[pl.ds(SUBLANES - j, lc), :]
    xe_ref[0:SUBLANES, :] = x[lc - SUBLANES:, :]

    ri = _sigmoid(jnp.dot(xc.astype(BF16), wri_ref[...], preferred_element_type=F32) + bri_ref[...])
    r = ri[:, :w]
    gi = ri[:, w:]
    log_a = -LRU_C * r * sp_ref[...]
    a = jnp.exp(log_a)
    mult = jnp.sqrt(1.0 - jnp.exp(2.0 * log_a))
    b = mult * (gi * xc)

    rowmod = lax.broadcasted_iota(jnp.int32, (lc, w), 0) % SUBLANES
    for dist in (1, 2, 4):
        keep = rowmod >= dist
        a_sh = jnp.where(keep, pltpu.roll(a, dist, 0), 1.0)
        b_sh = jnp.where(keep, pltpu.roll(b, dist, 0), 0.0)
        b = a * b_sh + b
        a = a * a_sh
    a_ref[...] = a
    b_ref[...] = b

    def body(r8, hprev):
        rows = pl.ds(pl.multiple_of(r8 * SUBLANES, SUBLANES), SUBLANES)
        h = a_ref[rows, :] * hprev + b_ref[rows, :]
        b_ref[rows, :] = h
        return h[SUBLANES - 1:SUBLANES, :]

    carry_ref[...] = lax.fori_loop(0, lc // SUBLANES, body, carry_ref[...])
    o_ref[...] = (b_ref[...] * _gelu(gate_ref[...].astype(F32))).astype(o_ref.dtype)


def _lru_mixer(proj, bsz, seq, conv_w, conv_b, w_r, b_r, w_i, b_i, lam, w, x_col, lc=512):
    lc = min(lc, seq)
    nc = seq // lc
    eye = jnp.eye(LRU_BLOCKS, dtype=F32)
    dense = lambda m: jnp.einsum('ncd,nm->ncmd', m, eye).reshape(w, w)
    wri = jnp.concatenate([dense(w_r), dense(w_i)], axis=1).astype(BF16)
    bri = jnp.concatenate([b_r, b_i]).reshape(1, 2 * w)
    sp = jax.nn.softplus(-lam.astype(F32)).reshape(1, w)
    xb = x_col // w
    c2 = lambda b, c: (0, 0)
    return pl.pallas_call(
        _lru_kernel,
        grid=(bsz, nc),
        in_specs=[
            pl.BlockSpec((lc, w), lambda b, c: (b * nc + c, xb)),
            pl.BlockSpec((lc, w), lambda b, c: (b * nc + c, xb + 1)),
            pl.BlockSpec((CONV_WIDTH, w), c2),
            pl.BlockSpec((1, w), c2),
            pl.BlockSpec((w, 2 * w), c2),
            pl.BlockSpec((1, 2 * w), c2),
            pl.BlockSpec((1, w), c2),
        ],
        out_specs=pl.BlockSpec((lc, w), lambda b, c: (b * nc + c, 0)),
        out_shape=jax.ShapeDtypeStruct((bsz * seq, w), ACT_DTYPE),
        scratch_shapes=[pltpu.VMEM((lc + SUBLANES, w), F32), pltpu.VMEM((lc, w), F32),
                        pltpu.VMEM((lc, w), F32), pltpu.VMEM((1, w), F32)],
        compiler_params=_params("arbitrary", "arbitrary"),
        name="rglru_mixer",
    )(proj, proj, conv_w, conv_b.reshape(1, w), wri, bri, sp)


def _merge_kernel(y0_ref, y1_ref, y2_ref, gl_ref, x_ref, wb_ref, wo_ref, o_ref):
    d = x_ref.shape[1]
    merged = None
    for n, y_ref in enumerate((y0_ref, y1_ref, y2_ref)):
        br = jnp.dot(y_ref[...].astype(BF16), wb_ref[n], preferred_element_type=F32)
        term = _sigmoid(gl_ref[:, n * d:(n + 1) * d].astype(F32)) * br
        merged = term if merged is None else merged + term
    o_ref[...] = x_ref[...] + jnp.dot(merged.astype(BF16), wo_ref[...], preferred_element_type=F32)


def _merge(ys, proj, x2, w_branch, w_out, gate_col, tm=512):
    t, d = x2.shape
    bw = ys[0].shape[1]
    tm = min(tm, t)
    gb = gate_col // (N_BRANCH * d)
    yspec = pl.BlockSpec((tm, bw), lambda i: (i, 0))
    return pl.pallas_call(
        _merge_kernel,
        grid=(t // tm,),
        in_specs=[yspec, yspec, yspec,
                  pl.BlockSpec((tm, N_BRANCH * d), lambda i: (i, gb)),
                  pl.BlockSpec((tm, d), lambda i: (i, 0)),
                  pl.BlockSpec((N_BRANCH, bw, d), lambda i: (0, 0, 0)),
                  pl.BlockSpec((d, d), lambda i: (0, 0))],
        out_specs=pl.BlockSpec((tm, d), lambda i: (i, 0)),
        out_shape=jax.ShapeDtypeStruct((t, d), F32),
        compiler_params=_params("parallel"),
        name="merge_out",
    )(ys[0], ys[1], ys[2], proj, x2, w_branch.astype(BF16), w_out.astype(BF16))


_NEG = -1e30
PEER_SUB = 512
BF16_ROWS = 16
PEER_FLAGS = None


def _top_rows(s, count, n_ranked=0):
    rows = []
    rank = None
    if n_ranked:
        rank = jnp.full(s.shape, float(n_ranked), F32)
    for i in range(count):
        m = jnp.max(s, axis=0, keepdims=True)
        rows.append(m)
        if i + 1 < count or i < n_ranked:
            hit = s >= m
            if i < n_ranked:
                rank = jnp.where(hit, float(i), rank)
            s = jnp.where(hit, _NEG, s)
    return (rows, rank) if n_ranked else rows


def _batcher_pairs(lo, hi):
    def merge(lo, hi, r):
        step = r * 2
        if step < hi - lo:
            yield from merge(lo, hi, step)
            yield from merge(lo + r, hi, step)
            for i in range(lo + r, hi - r, step):
                yield (i, i + r)
        else:
            yield (lo, lo + r)

    if hi - lo >= 1:
        mid = lo + (hi - lo) // 2
        yield from _batcher_pairs(lo, mid)
        yield from _batcher_pairs(mid + 1, hi)
        yield from merge(lo, hi, 1)


def _top16_chunks(s):
    nrow = s.shape[0] // SUBLANES
    chunks = [s[r * SUBLANES:(r + 1) * SUBLANES, :] for r in range(nrow)]
    c = list(chunks)
    for i, j in _batcher_pairs(0, nrow - 1):
        c[i], c[j] = jnp.maximum(c[i], c[j]), jnp.minimum(c[i], c[j])
    for shift in (4, 2, 1):
        c = [jnp.maximum(c[r], pltpu.roll(c[nrow - 1 - r], shift, 0)) for r in range(nrow)]
        dist = nrow // 2
        while dist >= 1:
            for i in range(nrow):
                if not i & dist:
                    c[i], c[i + dist] = jnp.maximum(c[i], c[i + dist]), jnp.minimum(c[i], c[i + dist])
            dist //= 2
    rest = None
    for ch in chunks:
        m = jnp.where(ch < c[nrow - 1], ch, _NEG)
        rest = m if rest is None else jnp.maximum(rest, m)
    for shift in (4, 2, 1):
        rest = jnp.maximum(rest, pltpu.roll(rest, shift, 0))
    return chunks, c, rest


def _gelu_gate(x):
    k1 = -2.0 * _SQRT_2_OVER_PI
    k3 = k1 * 0.044715
    return x / (1.0 + jnp.exp(x * (k1 + k3 * (x * x))))


def _peer_kernel(x_ref, nw_ref, wqt_ref, keys_ref, u_ref, vt_ref, fw_ref, o_ref,
                 xn_ref, qt_ref, rank_ref, e2_ref, cnt_ref, g_ref, acc_ref, *, final_norm):
    e = pl.program_id(1)
    ne = pl.num_programs(1)
    eb = u_ref.shape[0]
    tm = x_ref.shape[0]
    nk = PEER_NKEYS
    kk = PEER_TOPK + 1
    ntile = nk // BF16_ROWS

    @pl.when(e == 0)
    def _prep():
        xn = _rms(x_ref[...], nw_ref[...]).astype(BF16)
        xn_ref[...] = xn
        qt_ref[...] = lax.dot_general(wqt_ref[...], xn, (((1,), (1,)), ((), ())),
                                      preferred_element_type=F32)
        acc_ref[...] = jnp.zeros_like(acc_ref)

        def head(h, _):
            q1 = qt_ref[pl.ds(pl.multiple_of(h * 2 * PEER_HALF, PEER_HALF), PEER_HALF), :]
            q2 = qt_ref[pl.ds(pl.multiple_of(h * 2 * PEER_HALF + PEER_HALF, PEER_HALF), PEER_HALF), :]
            s1 = jnp.dot(keys_ref[2 * h], q1, preferred_element_type=F32)
            s2 = jnp.dot(keys_ref[2 * h + 1], q2, preferred_element_type=F32)
            ch1, t1, n1 = _top16_chunks(s1)
            ch2, t2, n2 = _top16_chunks(s2)
            v1 = [t[0:1, :] for t in t1] + [n1[0:1, :]]
            v2 = [t[0:1, :] for t in t2] + [n2[0:1, :]]
            cands = [v1[i] + v2[j] for i in range(kk) for j in range(kk) if (i + 1) * (j + 1) <= kk]
            cand = jnp.concatenate(cands, axis=0)
            top = _top_rows(cand, kk)
            theta = 0.5 * (top[PEER_TOPK - 1] + top[PEER_TOPK])
            cmax = v1[0] + v2[0]
            z = jnp.sum(jnp.where(cand >= theta, jnp.exp(cand - cmax), 0.0), axis=0, keepdims=True)
            theta8 = jnp.broadcast_to(theta, (SUBLANES, tm))
            zinv8 = jnp.broadcast_to(1.0 / z, (SUBLANES, tm))
            ranks, e2s = [], []
            for r, (c1, c2) in enumerate(zip(ch1, ch2)):
                rows = slice(r * SUBLANES, (r + 1) * SUBLANES)
                thr = theta8 - c1
                cnt = jnp.zeros_like(c1)
                rank = jnp.full(c2.shape, float(PEER_TOPK), F32)
                for j in range(PEER_TOPK):
                    cnt = jnp.where(t2[j] >= thr, float(j + 1), cnt)
                    jr = PEER_TOPK - 1 - j
                    rank = jnp.where(c2 >= t2[jr], float(jr), rank)
                cnt_ref[h, rows, :] = cnt
                g_ref[h, rows, :] = jnp.exp(c1 - t1[0]) * zinv8
                ranks.append(rank)
                e2s.append(jnp.exp(c2 - t2[0]))
            per = BF16_ROWS // SUBLANES
            for r in range(ntile):
                rank_ref[h, r] = jnp.concatenate(ranks[r * per:(r + 1) * per], axis=0).astype(BF16)
                e2_ref[h, r] = jnp.concatenate(e2s[r * per:(r + 1) * per], axis=0).astype(BF16)
            return 0

        lax.fori_loop(0, PEER_HEADS, head, 0)

    nsub = eb // PEER_SUB
    a_per = PEER_SUB // nk

    def gate(j):
        cnts, grs = {}, {}
        for ai in range(a_per):
            a = e * (eb // nk) + j * a_per + ai
            for h in range(PEER_HEADS):
                cnts[ai, h] = jnp.broadcast_to(cnt_ref[h, pl.ds(a, 1), :], (BF16_ROWS, tm)).astype(BF16)
                grs[ai, h] = jnp.broadcast_to(g_ref[h, pl.ds(a, 1), :], (BF16_ROWS, tm)).astype(BF16)
        wgts = [[None] * ntile for _ in range(a_per)]
        for r in range(ntile):
            for h in range(PEER_HEADS):
                rk = rank_ref[h, r]
                ee = e2_ref[h, r]
                for ai in range(a_per):
                    term = jnp.where(rk < cnts[ai, h], ee * grs[ai, h], 0.0)
                    wgts[ai][r] = term if h == 0 else wgts[ai][r] + term
        return wgts

    def score(j):
        return lax.dot_general(u_ref[j * PEER_SUB:(j + 1) * PEER_SUB, :], xn_ref[...],
                               (((1,), (1,)), ((), ())), preferred_element_type=F32)

    def hidden(act, wgts):
        parts = []
        for ai in range(a_per):
            gel = _gelu_gate(act[ai * nk:(ai + 1) * nk, :]).astype(BF16)
            for r in range(ntile):
                parts.append(gel[r * BF16_ROWS:(r + 1) * BF16_ROWS, :] * wgts[ai][r])
        return jnp.concatenate(parts, axis=0)

    hid = hidden(score(0), gate(0))
    for j in range(nsub):
        if j + 1 < nsub:
            wgts = gate(j + 1)
            act = score(j + 1)
        acc_ref[...] += jnp.dot(vt_ref[j], hid, preferred_element_type=F32)
        if j + 1 < nsub:
            hid = hidden(act, wgts)

    @pl.when(e == ne - 1)
    def _fin():
        out = x_ref[...] + acc_ref[...].T
        if final_norm:
            out = _rms(out, fw_ref[...])
        o_ref[...] = out


def _peer_values(v):
    n_exp, d = v.shape
    return v.astype(BF16).reshape(n_exp // PEER_SUB, PEER_SUB, d).transpose(0, 2, 1)


def _peer(x2, norm_w, w_query, sub_keys, u_bf16, vt_bf16, final_w, final_norm, tm=512, eb=2048):
    t, d = x2.shape
    n_exp = u_bf16.shape[0]
    tm = min(tm, t)
    wqt = w_query.T.astype(BF16)
    keys = sub_keys.reshape(PEER_HEADS * 2, PEER_NKEYS, PEER_HALF).astype(F32)
    c2 = lambda i, e: (0, 0)
    head_buf = pltpu.VMEM((PEER_HEADS, PEER_NKEYS, tm), F32)
    packed_buf = pltpu.VMEM((PEER_HEADS, PEER_NKEYS // BF16_ROWS, BF16_ROWS, tm), BF16)
    return pl.pallas_call(
        functools.partial(_peer_kernel, final_norm=final_norm),
        grid=(t // tm, n_exp // eb),
        in_specs=[
            pl.BlockSpec((tm, d), lambda i, e: (i, 0)),
            pl.BlockSpec((1, d), c2),
            pl.BlockSpec(wqt.shape, c2),
            pl.BlockSpec(keys.shape, lambda i, e: (0, 0, 0)),
            pl.BlockSpec((eb, d), lambda i, e: (e, 0)),
            pl.BlockSpec((eb // PEER_SUB, d, PEER_SUB), lambda i, e: (e, 0, 0)),
            pl.BlockSpec((1, d), c2),
        ],
        out_specs=pl.BlockSpec((tm, d), lambda i, e: (i, 0)),
        out_shape=jax.ShapeDtypeStruct((t, d), F32),
        scratch_shapes=[pltpu.VMEM((tm, d), BF16), pltpu.VMEM((wqt.shape[0], tm), F32),
                        packed_buf, packed_buf, head_buf, head_buf, pltpu.VMEM((d, tm), F32)],
        compiler_params=_params("parallel", "arbitrary", flags=PEER_FLAGS),
        name="peer_dense",
    )(x2, norm_w.reshape(1, d), wqt, keys, u_bf16, vt_bf16, final_w.reshape(1, d))


def kernel(x, positions, norm_mix, w_in, s5_lam_re, s5_lam_im, s5_log_dt, s5_b_re, s5_b_im, s5_c_re, s5_c_im, s5_d, s5_w_glu, s5_b_glu, ret_gn_w, lru_conv_w, lru_conv_b, lru_w_r, lru_b_r, lru_w_i, lru_b_i, lru_lam, w_branch, w_out, norm_ffn, peer_w_query, peer_sub_keys, peer_u, peer_v, final_norm):
    bsz, seq, d = x.shape
    depth = norm_mix.shape[0]
    s5_w = s5_d.shape[1]
    v_w = ret_gn_w.shape[1]
    qk_w = v_w // 2
    lru_w = lru_lam.shape[1]
    q_col = s5_w
    x_col = q_col + 2 * qk_w + 2 * v_w
    gate_col = x_col + 2 * lru_w

    x2 = x.reshape(bsz * seq, d)
    cos_t, sin_t = _rope_tables(positions, qk_w // RET_HEADS // 2)
    for l in range(depth):
        proj = _norm_proj(x2, norm_mix[l], w_in[l].astype(BF16))
        s5c = _s5_consts(s5_lam_re[l], s5_lam_im[l], s5_log_dt[l], s5_b_re[l], s5_b_im[l],
                         s5_c_re[l], s5_c_im[l])
        y_s5 = _s5_mixer(proj, bsz, seq, s5c, s5_d[l], s5_w_glu[l], s5_b_glu[l])
        y_ret = _ret_mixer(proj, bsz, seq, cos_t, sin_t, ret_gn_w[l], qk_w, v_w, q_col)
        y_lru = _lru_mixer(proj, bsz, seq, lru_conv_w[l], lru_conv_b[l], lru_w_r[l], lru_b_r[l],
                           lru_w_i[l], lru_b_i[l], lru_lam[l], lru_w, x_col)
        x2 = _merge((y_s5, y_ret, y_lru), proj, x2, w_branch[l], w_out[l], gate_col)
        x2 = _peer(x2, norm_ffn[l], peer_w_query[l], peer_sub_keys[l], peer_u[l].astype(BF16),
                   _peer_values(peer_v[l]), final_norm, final_norm=(l == depth - 1))
    return x2.reshape(bsz, seq, d)
```

```python
import functools
import math

import jax
import jax.numpy as jnp
import numpy as np
from jax import lax
from jax.experimental import pallas as pl
from jax.experimental.pallas import tpu as pltpu

F32 = jnp.float32
BF16 = jnp.bfloat16
ACT_DTYPE = BF16

EPS = 1e-6
S5_GROUP = 16
S5_STATE = 64
RET_HEADS = 4
RET_CHUNK = 128
ROPE_BASE = 10000.0
LRU_BLOCKS = 8
CONV_WIDTH = 4
LRU_C = 8.0
N_BRANCH = 3
PEER_HEADS = 8
PEER_NKEYS = 128
PEER_TOPK = 16
PEER_HALF = 64

SUBLANES = 8
LANES = 128
VMEM_LIMIT = 56 * 1024 * 1024

_SQRT_2_OVER_PI = math.sqrt(2.0 / math.pi)


def _gelu(x):
    return 0.5 * x * (1.0 + jnp.tanh(_SQRT_2_OVER_PI * (x + 0.044715 * (x * x * x))))


def _sigmoid(x):
    return 1.0 / (1.0 + jnp.exp(-x))


def _rms(x, w):
    return x * lax.rsqrt(jnp.mean(x * x, axis=-1, keepdims=True) + EPS) * w


def _params(*sem, flags=None):
    return pltpu.CompilerParams(dimension_semantics=sem, vmem_limit_bytes=VMEM_LIMIT, flags=flags)


def _norm_proj_kernel(x_ref, nw_ref, w_ref, o_ref, xn_ref):
    @pl.when(pl.program_id(1) == 0)
    def _():
        xn_ref[...] = _rms(x_ref[...], nw_ref[...]).astype(BF16)

    o_ref[...] = jnp.dot(xn_ref[...], w_ref[...], preferred_element_type=F32).astype(o_ref.dtype)


def _norm_proj(x2, norm_w, w_bf16, tm=1024, tn=3072):
    t, d = x2.shape
    n = w_bf16.shape[1]
    tm = min(tm, t)
    return pl.pallas_call(
        _norm_proj_kernel,
        grid=(t // tm, n // tn),
        in_specs=[
            pl.BlockSpec((tm, d), lambda i, j: (i, 0)),
            pl.BlockSpec((1, d), lambda i, j: (0, 0)),
            pl.BlockSpec((d, tn), lambda i, j: (0, j)),
        ],
        out_specs=pl.BlockSpec((tm, tn), lambda i, j: (i, j)),
        out_shape=jax.ShapeDtypeStruct((t, n), ACT_DTYPE),
        scratch_shapes=[pltpu.VMEM((tm, d), BF16)],
        compiler_params=_params("parallel", "arbitrary"),
        name="norm_proj",
    )(x2, norm_w.reshape(1, d), w_bf16)


S5_LANE_CHUNK = 512
S5_SPLIT = 2


def _s5_kernel(u_ref, bmat_ref, cmat_ref, sc_ref, d_ref, wglu_ref, bglu_ref, o_ref,
               bu_ref, carry_ref, *, ns):
    nh = ns // S5_SPLIT

    @pl.when(pl.program_id(1) == 0)
    def _():
        carry_ref[...] = jnp.zeros_like(carry_ref)

    u = u_ref[...].astype(F32)
    ub = u.astype(BF16)
    lc, w = u.shape
    for hh in range(S5_SPLIT):
        part = jnp.dot(ub[:, hh * (w // S5_SPLIT):(hh + 1) * (w // S5_SPLIT)], bmat_ref[hh],
                       preferred_element_type=F32)
        bu_ref[:, hh * nh:(hh + 1) * nh] = part[:, :nh]
        bu_ref[:, ns + hh * nh:ns + (hh + 1) * nh] = part[:, nh:]

    for c0 in range(0, ns, S5_LANE_CHUNK):
        re_sl = pl.ds(c0, S5_LANE_CHUNK)
        im_sl = pl.ds(ns + c0, S5_LANE_CHUNK)
        consts = [sc_ref[k, :, c0:c0 + S5_LANE_CHUNK] for k in range(8)]

        def body(r, carry, re_sl=re_sl, im_sl=im_sl, consts=consts):
            hpr, hpi = carry
            rows = pl.ds(pl.multiple_of(r * SUBLANES, SUBLANES), SUBLANES)
            br = bu_ref[rows, re_sl]
            bi = bu_ref[rows, im_sl]
            for lvl, dist in enumerate((1, 2, 4)):
                ar, ai = consts[2 * lvl], consts[2 * lvl + 1]
                sr = pltpu.roll(br, dist, 0)
                si = pltpu.roll(bi, dist, 0)
                br, bi = br + ar * sr - ai * si, bi + ar * si + ai * sr
            pr, pim = consts[6], consts[7]
            hr = br + pr * hpr - pim * hpi
            hi = bi + pr * hpi + pim * hpr
            bu_ref[rows, re_sl] = hr
            bu_ref[rows, im_sl] = hi
            return hr[SUBLANES - 1:SUBLANES, :], hi[SUBLANES - 1:SUBLANES, :]

        hr_last, hi_last = lax.fori_loop(
            0, lc // SUBLANES, body, (carry_ref[:, re_sl], carry_ref[:, im_sl]))
        carry_ref[:, re_sl] = hr_last
        carry_ref[:, im_sl] = hi_last

    ys = []
    for hh in range(S5_SPLIT):
        hs = jnp.concatenate([bu_ref[:, hh * nh:(hh + 1) * nh],
                              bu_ref[:, ns + hh * nh:ns + (hh + 1) * nh]], axis=1)
        ys.append(jnp.dot(hs.astype(BF16), cmat_ref[hh], preferred_element_type=F32))
    y = jnp.concatenate(ys, axis=1)
    z = _gelu_gate(y + d_ref[...] * u)
    gl = jnp.dot(z.astype(BF16), wglu_ref[...], preferred_element_type=F32) + bglu_ref[...]
    o_ref[...] = (z * _sigmoid(gl)).astype(o_ref.dtype)


def _s5_consts(lam_re, lam_im, log_dt, b_re, b_im, c_re, c_im):
    g, p = lam_re.shape
    h16 = b_re.shape[-1]
    lr, li = lam_re.astype(F32), lam_im.astype(F32)
    dt = jnp.broadcast_to(jnp.exp(log_dt.astype(F32))[:, None], lr.shape)

    def lam_bar_pow(lr, li, dt, n):
        mag = jnp.exp(lr * dt * n)
        return mag * jnp.cos(li * dt * n), mag * jnp.sin(li * dt * n)

    br_, bi_ = lam_bar_pow(lr, li, dt, 1.0)
    den = lr * lr + li * li
    cr = ((br_ - 1.0) * lr + bi_ * li) / den
    ci = (bi_ * lr - (br_ - 1.0) * li) / den
    bre, bim = b_re.astype(F32), b_im.astype(F32)
    bbar_re = cr[..., None] * bre - ci[..., None] * bim
    bbar_im = cr[..., None] * bim + ci[..., None] * bre
    gs = g // S5_SPLIT
    eye = jnp.eye(gs, dtype=F32)

    def bdiag_in(m):
        return jnp.einsum('gph,gk->ghkp', m, eye).reshape(gs * h16, gs * p)

    def bdiag_out(m):
        return jnp.einsum('ghp,gk->gpkh', m, eye).reshape(gs * p, gs * h16)

    halves = lambda m: [m[hh * gs:(hh + 1) * gs] for hh in range(S5_SPLIT)]
    bmat = jnp.stack([jnp.concatenate([bdiag_in(r), bdiag_in(i)], axis=1)
                      for r, i in zip(halves(bbar_re), halves(bbar_im))])
    cmat = jnp.stack([jnp.concatenate([bdiag_out(r), bdiag_out(i)], axis=0)
                      for r, i in zip(halves(c_re.astype(F32)), halves(-c_im.astype(F32)))])

    row = jnp.arange(SUBLANES, dtype=F32)[:, None]
    planes = []
    for dist in (1, 2, 4):
        pr, pi_ = lam_bar_pow(lr, li, dt, float(dist))
        keep = row >= dist
        planes += [jnp.where(keep, pr.reshape(1, g * p), 0.0), jnp.where(keep, pi_.reshape(1, g * p), 0.0)]
    flat = lambda m: m.reshape(1, g * p)
    pr, pi_ = lam_bar_pow(flat(lr), flat(li), flat(dt), row + 1.0)
    planes += [pr, pi_]
    return bmat.astype(BF16), cmat.astype(BF16), jnp.stack(planes).astype(F32)


def _s5_mixer(proj, bsz, seq, consts, d_skip, w_glu, b_glu, lc=512):
    bmat, cmat, sc = consts
    w = bmat.shape[0] * bmat.shape[1]
    ns = bmat.shape[0] * bmat.shape[2] // 2
    lc = min(lc, seq)
    nc = seq // lc
    const2 = lambda b, c: (0, 0)
    return pl.pallas_call(
        functools.partial(_s5_kernel, ns=ns),
        grid=(bsz, nc),
        in_specs=[
            pl.BlockSpec((lc, w), lambda b, c: (b * nc + c, 0)),
            pl.BlockSpec(bmat.shape, lambda b, c: (0, 0, 0)),
            pl.BlockSpec(cmat.shape, lambda b, c: (0, 0, 0)),
            pl.BlockSpec(sc.shape, lambda b, c: (0, 0, 0)),
            pl.BlockSpec((1, w), const2),
            pl.BlockSpec((w, w), const2),
            pl.BlockSpec((1, w), const2),
        ],
        out_specs=pl.BlockSpec((lc, w), lambda b, c: (b * nc + c, 0)),
        out_shape=jax.ShapeDtypeStruct((bsz * seq, w), ACT_DTYPE),
        scratch_shapes=[pltpu.VMEM((lc, 2 * ns), F32), pltpu.VMEM((1, 2 * ns), F32)],
        compiler_params=_params("arbitrary", "arbitrary"),
        name="s5_mixer",
    )(proj, bmat, cmat, sc, d_skip.reshape(1, w), w_glu.astype(BF16), b_glu.reshape(1, w))


def _rope_kernel(pos_ref, freq_ref, cos_ref, sin_ref):
    ang = pos_ref[...].astype(F32) * freq_ref[...]
    cos_ref[...] = jnp.cos(ang)
    sin_ref[...] = jnp.sin(ang)


def _rope_tables(positions, half, tm=2048):
    t = positions.size
    tm = min(tm, t)
    reps = LANES // half
    inv_freq = ROPE_BASE ** (-jnp.arange(half, dtype=F32) / half)
    freq = jnp.tile(inv_freq, reps).reshape(1, LANES)
    return pl.pallas_call(
        _rope_kernel,
        grid=(t // tm,),
        in_specs=[pl.BlockSpec((tm, 1), lambda i: (i, 0)), pl.BlockSpec((1, LANES), lambda i: (0, 0))],
        out_specs=[pl.BlockSpec((tm, LANES), lambda i: (i, 0))] * 2,
        out_shape=[jax.ShapeDtypeStruct((t, LANES), F32)] * 2,
        compiler_params=_params("parallel"),
        name="rope_tables",
    )(positions.reshape(t, 1), freq)


def _ret_kernel(q_ref, k_ref, v_ref, g_ref, cos_ref, sin_ref, dec_ref, zeta_ref, xi_ref, cd_ref,
                gnw_ref, o_ref, state_ref, *, dk, dv):
    @pl.when(pl.program_id(1) == 0)
    def _():
        state_ref[...] = jnp.zeros_like(state_ref)

    lt = q_ref.shape[0]
    half = dk // 2
    qkw = RET_HEADS * dk
    cos = jnp.concatenate([cos_ref[...]] * (qkw // LANES), axis=1)
    sin = jnp.concatenate([sin_ref[...]] * (qkw // LANES), axis=1)
    lane = lax.broadcasted_iota(jnp.int32, (lt, qkw), 1)
    first = (lane % dk) < half
    sin_signed = jnp.where(first, -sin, sin)

    def rot(x):
        swapped = jnp.where(first, pltpu.roll(x, qkw - half, 1), pltpu.roll(x, half, 1))
        return x * cos + swapped * sin_signed

    q = rot(q_ref[...].astype(F32))
    k = rot(k_ref[...].astype(F32)) * (dk ** -0.5)
    v = v_ref[...].astype(F32)
    g = g_ref[...].astype(F32)
    gnw = gnw_ref[...]

    for c0 in range(0, lt, RET_CHUNK):
        for h in range(RET_HEADS):
            qh = q[c0:c0 + RET_CHUNK, h * dk:(h + 1) * dk]
            kh = k[c0:c0 + RET_CHUNK, h * dk:(h + 1) * dk]
            vh = v[c0:c0 + RET_CHUNK, h * dv:(h + 1) * dv]
            st = state_ref[h]
            scores = lax.dot_general(qh, kh, (((1,), (1,)), ((), ())),
                                     preferred_element_type=F32) * dec_ref[h]
            o = jnp.dot(scores, vh, preferred_element_type=F32)
            o = o + jnp.dot(qh, st, preferred_element_type=F32) * xi_ref[h]
            kz = kh * zeta_ref[h]
            kv = lax.dot_general(kz, vh, (((0,), (0,)), ((), ())), preferred_element_type=F32)
            state_ref[h] = cd_ref[h] * st + kv
            mu = jnp.mean(o, axis=-1, keepdims=True)
            oc = o - mu
            var = jnp.mean(oc * oc, axis=-1, keepdims=True)
            on = oc * lax.rsqrt(var + EPS) * gnw[:, h * dv:(h + 1) * dv]
            gh = g[c0:c0 + RET_CHUNK, h * dv:(h + 1) * dv]
            o_ref[c0:c0 + RET_CHUNK, h * dv:(h + 1) * dv] = (gh * _sigmoid(gh) * on).astype(o_ref.dtype)


def _ret_consts(dk, dv):
    log_gamma = jnp.log1p(-(2.0 ** (-5.0 - jnp.arange(RET_HEADS, dtype=F32))))
    idx = jnp.arange(RET_CHUNK, dtype=F32)
    diff = idx[:, None] - idx[None, :]
    decay = jnp.where(diff >= 0, jnp.exp(log_gamma[:, None, None] * jnp.maximum(diff, 0.0)), 0.0)
    zeta = jnp.exp(log_gamma[:, None] * (RET_CHUNK - 1 - idx))[:, :, None]
    xi = jnp.exp(log_gamma[:, None] * (idx + 1.0))[:, :, None]
    cd = jnp.broadcast_to(jnp.exp(log_gamma * RET_CHUNK)[:, None, None], (RET_HEADS, dk, dv))
    return decay, jnp.broadcast_to(zeta, (RET_HEADS, RET_CHUNK, dk)), \
        jnp.broadcast_to(xi, (RET_HEADS, RET_CHUNK, dv)), cd


def _ret_mixer(proj, bsz, seq, cos_t, sin_t, gn_w, qk_w, v_w, q_col, lt=1024):
    dk = qk_w // RET_HEADS
    dv = v_w // RET_HEADS
    lt = min(lt, seq)
    nc = seq // lt
    decay, zeta, xi, cd = _ret_consts(dk, dv)
    qb = q_col // qk_w
    vb = (q_col + 2 * qk_w) // v_w
    row = lambda b, c: b * nc + c
    c3 = lambda b, c: (0, 0, 0)
    return pl.pallas_call(
        functools.partial(_ret_kernel, dk=dk, dv=dv),
        grid=(bsz, nc),
        in_specs=[
            pl.BlockSpec((lt, qk_w), lambda b, c: (row(b, c), qb)),
            pl.BlockSpec((lt, qk_w), lambda b, c: (row(b, c), qb + 1)),
            pl.BlockSpec((lt, v_w), lambda b, c: (row(b, c), vb)),
            pl.BlockSpec((lt, v_w), lambda b, c: (row(b, c), vb + 1)),
            pl.BlockSpec((lt, LANES), lambda b, c: (row(b, c), 0)),
            pl.BlockSpec((lt, LANES), lambda b, c: (row(b, c), 0)),
            pl.BlockSpec(decay.shape, c3),
            pl.BlockSpec(zeta.shape, c3),
            pl.BlockSpec(xi.shape, c3),
            pl.BlockSpec(cd.shape, c3),
            pl.BlockSpec((1, v_w), lambda b, c: (0, 0)),
        ],
        out_specs=pl.BlockSpec((lt, v_w), lambda b, c: (row(b, c), 0)),
        out_shape=jax.ShapeDtypeStruct((bsz * seq, v_w), ACT_DTYPE),
        scratch_shapes=[pltpu.VMEM((RET_HEADS, dk, dv), F32)],
        compiler_params=_params("arbitrary", "arbitrary"),
        name="retention_mixer",
    )(proj, proj, proj, proj, cos_t, sin_t, decay, zeta, xi, cd, gn_w.reshape(1, v_w))


def _lru_kernel(x_ref, gate_ref, cw_ref, cb_ref, wri_ref, bri_ref, sp_ref, o_ref,
                xe_ref, a_ref, b_ref, carry_ref):
    lc, w = x_ref.shape

    @pl.when(pl.program_id(1) == 0)
    def _():
        xe_ref[0:SUBLANES, :] = jnp.zeros((SUBLANES, w), F32)
        carry_ref[...] = jnp.zeros_like(carry_ref)

    x = x_ref[...].astype(F32)
    xe_ref[SUBLANES:, :] = x
    xc = cw_ref[CONV_WIDTH - 1:CONV_WIDTH, :] * x + cb_ref[...]
    for j in range(1, CONV_WIDTH):
        xc = xc + cw_ref[CONV_WIDTH - 1 - j:CONV_WIDTH - j, :] * xe_ref[pl.ds(SUBLANES - j, lc), :]
    xe_ref[0:SUBLANES, :] = x[lc - SUBLANES:, :]

    ri = _sigmoid(jnp.dot(xc.astype(BF16), wri_ref[...], preferred_element_type=F32) + bri_ref[...])
    r = ri[:, :w]
    gi = ri[:, w:]
    log_a = -LRU_C * r * sp_ref[...]
    a = jnp.exp(log_a)
    mult = jnp.sqrt(1.0 - a * a)
    b = mult * (gi * xc)

    rowmod = lax.broadcasted_iota(jnp.int32, (lc, w), 0) % SUBLANES
    for dist in (1, 2, 4):
        keep = rowmod >= dist
        a_sh = jnp.where(keep, pltpu.roll(a, dist, 0), 1.0)
        b_sh = jnp.where(keep, pltpu.roll(b, dist, 0), 0.0)
        b = a * b_sh + b
        a = a * a_sh
    a_ref[...] = a
    b_ref[...] = b

    def body(r8, hprev):
        rows = pl.ds(pl.multiple_of(r8 * SUBLANES, SUBLANES), SUBLANES)
        h = a_ref[rows, :] * hprev + b_ref[rows, :]
        b_ref[rows, :] = h
        return h[SUBLANES - 1:SUBLANES, :]

    carry_ref[...] = lax.fori_loop(0, lc // SUBLANES, body, carry_ref[...])
    o_ref[...] = (b_ref[...] * _gelu_gate(gate_ref[...].astype(F32))).astype(o_ref.dtype)


def _lru_mixer(proj, bsz, seq, conv_w, conv_b, w_r, b_r, w_i, b_i, lam, w, x_col, lc=1024):
    lc = min(lc, seq)
    nc = seq // lc
    eye = jnp.eye(LRU_BLOCKS, dtype=F32)
    dense = lambda m: jnp.einsum('ncd,nm->ncmd', m, eye).reshape(w, w)
    wri = jnp.concatenate([dense(w_r), dense(w_i)], axis=1).astype(BF16)
    bri = jnp.concatenate([b_r, b_i]).reshape(1, 2 * w)
    sp = jax.nn.softplus(-lam.astype(F32)).reshape(1, w)
    xb = x_col // w
    c2 = lambda b, c: (0, 0)
    return pl.pallas_call(
        _lru_kernel,
        grid=(bsz, nc),
        in_specs=[
            pl.BlockSpec((lc, w), lambda b, c: (b * nc + c, xb)),
            pl.BlockSpec((lc, w), lambda b, c: (b * nc + c, xb + 1)),
            pl.BlockSpec((CONV_WIDTH, w), c2),
            pl.BlockSpec((1, w), c2),
            pl.BlockSpec((w, 2 * w), c2),
            pl.BlockSpec((1, 2 * w), c2),
            pl.BlockSpec((1, w), c2),
        ],
        out_specs=pl.BlockSpec((lc, w), lambda b, c: (b * nc + c, 0)),
        out_shape=jax.ShapeDtypeStruct((bsz * seq, w), ACT_DTYPE),
        scratch_shapes=[pltpu.VMEM((lc + SUBLANES, w), F32), pltpu.VMEM((lc, w), F32),
                        pltpu.VMEM((lc, w), F32), pltpu.VMEM((1, w), F32)],
        compiler_params=_params("arbitrary", "arbitrary"),
        name="rglru_mixer",
    )(proj, proj, conv_w, conv_b.reshape(1, w), wri, bri, sp)


def _merge_kernel(y0_ref, y1_ref, y2_ref, gl_ref, x_ref, wb_ref, wo_ref, o_ref):
    d = x_ref.shape[1]
    merged = None
    for n, y_ref in enumerate((y0_ref, y1_ref, y2_ref)):
        br = jnp.dot(y_ref[...].astype(BF16), wb_ref[n], preferred_element_type=F32)
        term = _sigmoid(gl_ref[:, n * d:(n + 1) * d].astype(F32)) * br
        merged = term if merged is None else merged + term
    o_ref[...] = x_ref[...] + jnp.dot(merged.astype(BF16), wo_ref[...], preferred_element_type=F32)


def _merge(ys, proj, x2, w_branch, w_out, gate_col, tm=512):
    t, d = x2.shape
    bw = ys[0].shape[1]
    tm = min(tm, t)
    gb = gate_col // (N_BRANCH * d)
    yspec = pl.BlockSpec((tm, bw), lambda i: (i, 0))
    return pl.pallas_call(
        _merge_kernel,
        grid=(t // tm,),
        in_specs=[yspec, yspec, yspec,
                  pl.BlockSpec((tm, N_BRANCH * d), lambda i: (i, gb)),
                  pl.BlockSpec((tm, d), lambda i: (i, 0)),
                  pl.BlockSpec((N_BRANCH, bw, d), lambda i: (0, 0, 0)),
                  pl.BlockSpec((d, d), lambda i: (0, 0))],
        out_specs=pl.BlockSpec((tm, d), lambda i: (i, 0)),
        out_shape=jax.ShapeDtypeStruct((t, d), F32),
        compiler_params=_params("parallel"),
        name="merge_out",
    )(ys[0], ys[1], ys[2], proj, x2, w_branch.astype(BF16), w_out.astype(BF16))


_NEG = -1e30
PEER_SUB = 512
BF16_ROWS = 16
PEER_FLAGS = None


def _top_rows(s, count, n_ranked=0):
    rows = []
    rank = None
    if n_ranked:
        rank = jnp.full(s.shape, float(n_ranked), F32)
    for i in range(count):
        m = jnp.max(s, axis=0, keepdims=True)
        rows.append(m)
        if i + 1 < count or i < n_ranked:
            hit = s >= m
            if i < n_ranked:
                rank = jnp.where(hit, float(i), rank)
            s = jnp.where(hit, _NEG, s)
    return (rows, rank) if n_ranked else rows


def _batcher_pairs(lo, hi):
    def merge(lo, hi, r):
        step = r * 2
        if step < hi - lo:
            yield from merge(lo, hi, step)
            yield from merge(lo + r, hi, step)
            for i in range(lo + r, hi - r, step):
                yield (i, i + r)
        else:
            yield (lo, lo + r)

    if hi - lo >= 1:
        mid = lo + (hi - lo) // 2
        yield from _batcher_pairs(lo, mid)
        yield from _batcher_pairs(mid + 1, hi)
        yield from merge(lo, hi, 1)


def _top16_chunks(s):
    nrow = s.shape[0] // SUBLANES
    chunks = [s[r * SUBLANES:(r + 1) * SUBLANES, :] for r in range(nrow)]
    c = list(chunks)
    for i, j in _batcher_pairs(0, nrow - 1):
        c[i], c[j] = jnp.maximum(c[i], c[j]), jnp.minimum(c[i], c[j])
    for shift in (4, 2, 1):
        c = [jnp.maximum(c[r], pltpu.roll(c[nrow - 1 - r], shift, 0)) for r in range(nrow)]
        dist = nrow // 2
        while dist >= 1:
            for i in range(nrow):
                if not i & dist:
                    c[i], c[i + dist] = jnp.maximum(c[i], c[i + dist]), jnp.minimum(c[i], c[i + dist])
            dist //= 2
    rest = None
    for ch in chunks:
        m = jnp.where(ch < c[nrow - 1], ch, _NEG)
        rest = m if rest is None else jnp.maximum(rest, m)
    for shift in (4, 2, 1):
        rest = jnp.maximum(rest, pltpu.roll(rest, shift, 0))
    return chunks, c, rest


def _gelu_gate(x):
    k1 = -2.0 * _SQRT_2_OVER_PI * math.log2(math.e)
    k3 = k1 * 0.044715
    return x / (1.0 + jnp.exp2(x * (k1 + k3 * (x * x))))


def _peer_kernel(x_ref, nw_ref, wqt_ref, keys_ref, u_ref, vt_ref, fw_ref, o_ref,
                 xn_ref, qt_ref, rank_ref, e2_ref, cnt_ref, g_ref, acc_ref, *, final_norm):
    e = pl.program_id(1)
    ne = pl.num_programs(1)
    eb = u_ref.shape[0]
    tm = x_ref.shape[0]
    nk = PEER_NKEYS
    kk = PEER_TOPK + 1
    ntile = nk // BF16_ROWS

    @pl.when(e == 0)
    def _prep():
        xn = _rms(x_ref[...], nw_ref[...]).astype(BF16)
        xn_ref[...] = xn
        qt_ref[...] = lax.dot_general(wqt_ref[...], xn, (((1,), (1,)), ((), ())),
                                      preferred_element_type=F32)
        acc_ref[...] = jnp.zeros_like(acc_ref)

        def head(h, _):
            q1 = qt_ref[pl.ds(pl.multiple_of(h * 2 * PEER_HALF, PEER_HALF), PEER_HALF), :]
            q2 = qt_ref[pl.ds(pl.multiple_of(h * 2 * PEER_HALF + PEER_HALF, PEER_HALF), PEER_HALF), :]
            s1 = jnp.dot(keys_ref[2 * h], q1, preferred_element_type=F32)
            s2 = jnp.dot(keys_ref[2 * h + 1], q2, preferred_element_type=F32)
            ch1, t1, n1 = _top16_chunks(s1)
            ch2, t2, n2 = _top16_chunks(s2)
            v1 = [t[0:1, :] for t in t1] + [n1[0:1, :]]
            v2 = [t[0:1, :] for t in t2] + [n2[0:1, :]]
            cands = [v1[i] + v2[j] for i in range(kk) for j in range(kk) if (i + 1) * (j + 1) <= kk]
            cand = jnp.concatenate(cands, axis=0)
            top = _top_rows(cand, kk)
            theta = 0.5 * (top[PEER_TOPK - 1] + top[PEER_TOPK])
            cmax = v1[0] + v2[0]
            z = jnp.sum(jnp.where(cand >= theta, jnp.exp(cand - cmax), 0.0), axis=0, keepdims=True)
            theta8 = jnp.broadcast_to(theta, (SUBLANES, tm))
            zinv8 = jnp.broadcast_to(1.0 / z, (SUBLANES, tm))
            ranks, e2s = [], []
            for r, (c1, c2) in enumerate(zip(ch1, ch2)):
                rows = slice(r * SUBLANES, (r + 1) * SUBLANES)
                thr = theta8 - c1
                cnt = jnp.zeros_like(c1)
                rank = jnp.full(c2.shape, float(PEER_TOPK), F32)
                for j in range(PEER_TOPK):
                    cnt = jnp.where(t2[j] >= thr, float(j + 1), cnt)
                    jr = PEER_TOPK - 1 - j
                    rank = jnp.where(c2 >= t2[jr], float(jr), rank)
                cnt_ref[h, rows, :] = cnt
                g_ref[h, rows, :] = jnp.exp(c1 - t1[0]) * zinv8
                ranks.append(rank)
                e2s.append(jnp.exp(c2 - t2[0]))
            per = BF16_ROWS // SUBLANES
            for r in range(ntile):
                rank_ref[h, r] = jnp.concatenate(ranks[r * per:(r + 1) * per], axis=0).astype(BF16)
                e2_ref[h, r] = jnp.concatenate(e2s[r * per:(r + 1) * per], axis=0).astype(BF16)
            return 0

        lax.fori_loop(0, PEER_HEADS, head, 0)

    nsub = eb // PEER_SUB
    a_per = PEER_SUB // nk

    def gate(j):
        cnts, grs = {}, {}
        for ai in range(a_per):
            a = e * (eb // nk) + j * a_per + ai
            for h in range(PEER_HEADS):
                cnts[ai, h] = jnp.broadcast_to(cnt_ref[h, pl.ds(a, 1), :], (BF16_ROWS, tm)).astype(BF16)
                grs[ai, h] = jnp.broadcast_to(g_ref[h, pl.ds(a, 1), :], (BF16_ROWS, tm)).astype(BF16)
        wgts = [[None] * ntile for _ in range(a_per)]
        for r in range(ntile):
            for h in range(PEER_HEADS):
                rk = rank_ref[h, r]
                ee = e2_ref[h, r]
                for ai in range(a_per):
                    term = jnp.where(rk < cnts[ai, h], ee * grs[ai, h], 0.0)
                    wgts[ai][r] = term if h == 0 else wgts[ai][r] + term
        return wgts

    def score(j):
        return lax.dot_general(u_ref[j * PEER_SUB:(j + 1) * PEER_SUB, :], xn_ref[...],
                               (((1,), (1,)), ((), ())), preferred_element_type=F32)

    def hidden(act, wgts):
        parts = []
        for ai in range(a_per):
            gel = _gelu_gate(act[ai * nk:(ai + 1) * nk, :]).astype(BF16)
            for r in range(ntile):
                parts.append(gel[r * BF16_ROWS:(r + 1) * BF16_ROWS, :] * wgts[ai][r])
        return jnp.concatenate(parts, axis=0)

    hid = hidden(score(0), gate(0))
    for j in range(nsub):
        if j + 1 < nsub:
            wgts = gate(j + 1)
            act = score(j + 1)
        acc_ref[...] += jnp.dot(vt_ref[j], hid, preferred_element_type=F32)
        if j + 1 < nsub:
            hid = hidden(act, wgts)

    @pl.when(e == ne - 1)
    def _fin():
        out = x_ref[...] + acc_ref[...].T
        if final_norm:
            out = _rms(out, fw_ref[...])
        o_ref[...] = out


def _peer_values(v):
    n_exp, d = v.shape
    return v.astype(BF16).reshape(n_exp // PEER_SUB, PEER_SUB, d).transpose(0, 2, 1)


def _peer(x2, norm_w, w_query, sub_keys, u_bf16, vt_bf16, final_w, final_norm, tm=512, eb=2048):
    t, d = x2.shape
    n_exp = u_bf16.shape[0]
    tm = min(tm, t)
    wqt = w_query.T.astype(BF16)
    keys = sub_keys.reshape(PEER_HEADS * 2, PEER_NKEYS, PEER_HALF).astype(F32)
    c2 = lambda i, e: (0, 0)
    head_buf = pltpu.VMEM((PEER_HEADS, PEER_NKEYS, tm), F32)
    packed_buf = pltpu.VMEM((PEER_HEADS, PEER_NKEYS // BF16_ROWS, BF16_ROWS, tm), BF16)
    return pl.pallas_call(
        functools.partial(_peer_kernel, final_norm=final_norm),
        grid=(t // tm, n_exp // eb),
        in_specs=[
            pl.BlockSpec((tm, d), lambda i, e: (i, 0)),
            pl.BlockSpec((1, d), c2),
            pl.BlockSpec(wqt.shape, c2),
            pl.BlockSpec(keys.shape, lambda i, e: (0, 0, 0)),
            pl.BlockSpec((eb, d), lambda i, e: (e, 0)),
            pl.BlockSpec((eb // PEER_SUB, d, PEER_SUB), lambda i, e: (e, 0, 0)),
            pl.BlockSpec((1, d), c2),
        ],
        out_specs=pl.BlockSpec((tm, d), lambda i, e: (i, 0)),
        out_shape=jax.ShapeDtypeStruct((t, d), F32),
        scratch_shapes=[pltpu.VMEM((tm, d), BF16), pltpu.VMEM((wqt.shape[0], tm), F32),
                        packed_buf, packed_buf, head_buf, head_buf, pltpu.VMEM((d, tm), F32)],
        compiler_params=_params("parallel", "arbitrary", flags=PEER_FLAGS),
        name="peer_dense",
    )(x2, norm_w.reshape(1, d), wqt, keys, u_bf16, vt_bf16, final_w.reshape(1, d))


def kernel(x, positions, norm_mix, w_in, s5_lam_re, s5_lam_im, s5_log_dt, s5_b_re, s5_b_im, s5_c_re, s5_c_im, s5_d, s5_w_glu, s5_b_glu, ret_gn_w, lru_conv_w, lru_conv_b, lru_w_r, lru_b_r, lru_w_i, lru_b_i, lru_lam, w_branch, w_out, norm_ffn, peer_w_query, peer_sub_keys, peer_u, peer_v, final_norm):
    bsz, seq, d = x.shape
    depth = norm_mix.shape[0]
    s5_w = s5_d.shape[1]
    v_w = ret_gn_w.shape[1]
    qk_w = v_w // 2
    lru_w = lru_lam.shape[1]
    q_col = s5_w
    x_col = q_col + 2 * qk_w + 2 * v_w
    gate_col = x_col + 2 * lru_w

    x2 = x.reshape(bsz * seq, d)
    cos_t, sin_t = _rope_tables(positions, qk_w // RET_HEADS // 2)
    for l in range(depth):
        proj = _norm_proj(x2, norm_mix[l], w_in[l].astype(BF16))
        s5c = _s5_consts(s5_lam_re[l], s5_lam_im[l], s5_log_dt[l], s5_b_re[l], s5_b_im[l],
                         s5_c_re[l], s5_c_im[l])
        y_s5 = _s5_mixer(proj, bsz, seq, s5c, s5_d[l], s5_w_glu[l], s5_b_glu[l])
        y_ret = _ret_mixer(proj, bsz, seq, cos_t, sin_t, ret_gn_w[l], qk_w, v_w, q_col)
        y_lru = _lru_mixer(proj, bsz, seq, lru_conv_w[l], lru_conv_b[l], lru_w_r[l], lru_b_r[l],
                           lru_w_i[l], lru_b_i[l], lru_lam[l], lru_w, x_col)
        x2 = _merge((y_s5, y_ret, y_lru), proj, x2, w_branch[l], w_out[l], gate_col)
        x2 = _peer(x2, norm_ffn[l], peer_w_query[l], peer_sub_keys[l], peer_u[l].astype(BF16),
                   _peer_values(peer_v[l]), final_norm, final_norm=(l == depth - 1))
    return x2.reshape(bsz, seq, d)
```

```python
import functools
import math

import jax
import jax.numpy as jnp
import numpy as np
from jax import lax
from jax.experimental import pallas as pl
from jax.experimental.pallas import tpu as pltpu

F32 = jnp.float32
BF16 = jnp.bfloat16
ACT_DTYPE = BF16

EPS = 1e-6
S5_GROUP = 16
S5_STATE = 64
RET_HEADS = 4
RET_CHUNK = 128
ROPE_BASE = 10000.0
LRU_BLOCKS = 8
CONV_WIDTH = 4
LRU_C = 8.0
N_BRANCH = 3
PEER_HEADS = 8
PEER_NKEYS = 128
PEER_TOPK = 16
PEER_HALF = 64

SUBLANES = 8
LANES = 128
VMEM_LIMIT = 56 * 1024 * 1024

_SQRT_2_OVER_PI = math.sqrt(2.0 / math.pi)


def _gelu(x):
    return 0.5 * x * (1.0 + jnp.tanh(_SQRT_2_OVER_PI * (x + 0.044715 * (x * x * x))))


def _sigmoid(x):
    return 1.0 / (1.0 + jnp.exp(-x))


def _rms(x, w):
    return x * lax.rsqrt(jnp.mean(x * x, axis=-1, keepdims=True) + EPS) * w


def _params(*sem, flags=None):
    return pltpu.CompilerParams(dimension_semantics=sem, vmem_limit_bytes=VMEM_LIMIT, flags=flags)


def _norm_proj_kernel(x_ref, nw_ref, w_ref, o_ref, xn_ref):
    @pl.when(pl.program_id(1) == 0)
    def _():
        xn_ref[...] = _rms(x_ref[...], nw_ref[...]).astype(BF16)

    o_ref[...] = jnp.dot(xn_ref[...], w_ref[...], preferred_element_type=F32).astype(o_ref.dtype)


def _norm_proj(x2, norm_w, w_bf16, tm=1024, tn=3072):
    t, d = x2.shape
    n = w_bf16.shape[1]
    tm = min(tm, t)
    return pl.pallas_call(
        _norm_proj_kernel,
        grid=(t // tm, n // tn),
        in_specs=[
            pl.BlockSpec((tm, d), lambda i, j: (i, 0)),
            pl.BlockSpec((1, d), lambda i, j: (0, 0)),
            pl.BlockSpec((d, tn), lambda i, j: (0, j)),
        ],
        out_specs=pl.BlockSpec((tm, tn), lambda i, j: (i, j)),
        out_shape=jax.ShapeDtypeStruct((t, n), ACT_DTYPE),
        scratch_shapes=[pltpu.VMEM((tm, d), BF16)],
        compiler_params=_params("parallel", "arbitrary"),
        name="norm_proj",
    )(x2, norm_w.reshape(1, d), w_bf16)


S5_LANE_CHUNK = 512
S5_SPLIT = 2
S5_CHUNK = 512
S5_SEGS = SUBLANES


def _s5_kernel(u_ref, bmat_ref, cmat_ref, lam_ref, lamseg_ref, pw_ref, d_ref, wglu_ref, bglu_ref, o_ref,
               bu_ref, carry_ref, *, ns):
    nh = ns // S5_SPLIT

    @pl.when(pl.program_id(1) == 0)
    def _():
        carry_ref[...] = jnp.zeros_like(carry_ref)

    u = u_ref[...].astype(F32)
    ub = u.astype(BF16)
    lc, w = u.shape
    seg_len = lc // S5_SEGS
    for hh in range(S5_SPLIT):
        part = jnp.dot(ub[:, hh * (w // S5_SPLIT):(hh + 1) * (w // S5_SPLIT)], bmat_ref[hh],
                       preferred_element_type=F32)
        bu_ref[:, hh * nh:(hh + 1) * nh] = part[:, :nh]
        bu_ref[:, ns + hh * nh:ns + (hh + 1) * nh] = part[:, nh:]

    for c0 in range(0, ns, S5_LANE_CHUNK):
        cols = slice(c0, c0 + S5_LANE_CHUNK)
        re_sl = pl.ds(c0, S5_LANE_CHUNK)
        im_sl = pl.ds(ns + c0, S5_LANE_CHUNK)
        ar, ai = lam_ref[0, :, cols], lam_ref[1, :, cols]

        def local(t, h, re_sl=re_sl, im_sl=im_sl, ar=ar, ai=ai):
            hr, hi = h
            rows = pl.ds(pl.multiple_of(t * S5_SEGS, S5_SEGS), S5_SEGS)
            nr = ar * hr - ai * hi + bu_ref[rows, re_sl]
            ni = ar * hi + ai * hr + bu_ref[rows, im_sl]
            bu_ref[rows, re_sl] = nr
            bu_ref[rows, im_sl] = ni
            return nr, ni

        zero = jnp.zeros((S5_SEGS, S5_LANE_CHUNK), F32)
        er, ei = lax.fori_loop(0, seg_len, local, (zero, zero))

        sr, si = lamseg_ref[0, :, cols], lamseg_ref[1, :, cols]
        hin_r, hin_i = [carry_ref[:, re_sl]], [carry_ref[:, im_sl]]
        for k in range(S5_SEGS):
            pr, pim = hin_r[k], hin_i[k]
            hin_r.append(er[k:k + 1, :] + sr * pr - si * pim)
            hin_i.append(ei[k:k + 1, :] + sr * pim + si * pr)
        carry_ref[:, re_sl] = hin_r[S5_SEGS]
        carry_ref[:, im_sl] = hin_i[S5_SEGS]
        hr_in = jnp.concatenate(hin_r[:S5_SEGS], axis=0)
        hi_in = jnp.concatenate(hin_i[:S5_SEGS], axis=0)

        def fix(t, carry, re_sl=re_sl, im_sl=im_sl, hr_in=hr_in, hi_in=hi_in):
            rows = pl.ds(pl.multiple_of(t * S5_SEGS, S5_SEGS), S5_SEGS)
            pr = pw_ref[rows, re_sl]
            pim = pw_ref[rows, im_sl]
            bu_ref[rows, re_sl] = bu_ref[rows, re_sl] + pr * hr_in - pim * hi_in
            bu_ref[rows, im_sl] = bu_ref[rows, im_sl] + pr * hi_in + pim * hr_in
            return carry

        lax.fori_loop(0, seg_len, fix, 0)

    ys = []
    for hh in range(S5_SPLIT):
        hs = jnp.concatenate([bu_ref[:, hh * nh:(hh + 1) * nh],
                              bu_ref[:, ns + hh * nh:ns + (hh + 1) * nh]], axis=1)
        ys.append(jnp.dot(hs.astype(BF16), cmat_ref[hh], preferred_element_type=F32))
    y = jnp.concatenate(ys, axis=1)
    z = _gelu_gate(y + d_ref[...] * u)
    gl = jnp.dot(z.astype(BF16), wglu_ref[...], preferred_element_type=F32) + bglu_ref[...]
    o_ref[...] = (z * _sigmoid(gl)).astype(o_ref.dtype)


def _s5_consts(lam_re, lam_im, log_dt, b_re, b_im, c_re, c_im, seg_len):
    g, p = lam_re.shape
    h16 = b_re.shape[-1]
    lr, li = lam_re.astype(F32), lam_im.astype(F32)
    dt = jnp.broadcast_to(jnp.exp(log_dt.astype(F32))[:, None], lr.shape)

    def lam_bar_pow(lr, li, dt, n):
        mag = jnp.exp(lr * dt * n)
        return mag * jnp.cos(li * dt * n), mag * jnp.sin(li * dt * n)

    br_, bi_ = lam_bar_pow(lr, li, dt, 1.0)
    den = lr * lr + li * li
    cr = ((br_ - 1.0) * lr + bi_ * li) / den
    ci = (bi_ * lr - (br_ - 1.0) * li) / den
    bre, bim = b_re.astype(F32), b_im.astype(F32)
    bbar_re = cr[..., None] * bre - ci[..., None] * bim
    bbar_im = cr[..., None] * bim + ci[..., None] * bre
    gs = g // S5_SPLIT
    eye = jnp.eye(gs, dtype=F32)

    def bdiag_in(m):
        return jnp.einsum('gph,gk->ghkp', m, eye).reshape(gs * h16, gs * p)

    def bdiag_out(m):
        return jnp.einsum('ghp,gk->gpkh', m, eye).reshape(gs * p, gs * h16)

    halves = lambda m: [m[hh * gs:(hh + 1) * gs] for hh in range(S5_SPLIT)]
    bmat = jnp.stack([jnp.concatenate([bdiag_in(r), bdiag_in(i)], axis=1)
                      for r, i in zip(halves(bbar_re), halves(bbar_im))])
    cmat = jnp.stack([jnp.concatenate([bdiag_out(r), bdiag_out(i)], axis=0)
                      for r, i in zip(halves(c_re.astype(F32)), halves(-c_im.astype(F32)))])

    flat = lambda m: m.reshape(1, g * p)
    lam1 = lam_bar_pow(flat(lr), flat(li), flat(dt), 1.0)
    lam = jnp.stack([jnp.broadcast_to(m, (S5_SEGS, g * p)) for m in lam1])
    lamseg = jnp.stack(lam_bar_pow(flat(lr), flat(li), flat(dt), float(seg_len)))
    steps = jnp.arange(1, seg_len + 1, dtype=F32)[:, None]
    pw = jnp.concatenate(lam_bar_pow(flat(lr), flat(li), flat(dt), steps), axis=1)
    pw = jnp.repeat(pw, S5_SEGS, axis=0)
    return bmat.astype(BF16), cmat.astype(BF16), lam.astype(F32), lamseg.astype(F32), pw.astype(F32)


def _s5_mixer(proj, bsz, seq, consts, d_skip, w_glu, b_glu):
    bmat, cmat, lam, lamseg, pw = consts
    w = bmat.shape[0] * bmat.shape[1]
    ns = bmat.shape[0] * bmat.shape[2] // 2
    lc = pw.shape[0]
    nc = seq // lc

    def to_segments(m, rows_a, rows_b):
        return m.reshape(bsz, nc, rows_a, rows_b, w).transpose(0, 1, 3, 2, 4).reshape(bsz * seq, w)

    u = to_segments(proj[:, :w], S5_SEGS, lc // S5_SEGS)
    const2 = lambda b, c: (0, 0)
    const3 = lambda b, c: (0, 0, 0)
    out = pl.pallas_call(
        functools.partial(_s5_kernel, ns=ns),
        grid=(bsz, nc),
        in_specs=[
            pl.BlockSpec((lc, w), lambda b, c: (b * nc + c, 0)),
            pl.BlockSpec(bmat.shape, const3),
            pl.BlockSpec(cmat.shape, const3),
            pl.BlockSpec(lam.shape, const3),
            pl.BlockSpec(lamseg.shape, const3),
            pl.BlockSpec(pw.shape, const2),
            pl.BlockSpec((1, w), const2),
            pl.BlockSpec((w, w), const2),
            pl.BlockSpec((1, w), const2),
        ],
        out_specs=pl.BlockSpec((lc, w), lambda b, c: (b * nc + c, 0)),
        out_shape=jax.ShapeDtypeStruct((bsz * seq, w), ACT_DTYPE),
        scratch_shapes=[pltpu.VMEM((lc, 2 * ns), F32), pltpu.VMEM((1, 2 * ns), F32)],
        compiler_params=_params("arbitrary", "arbitrary"),
        name="s5_mixer",
    )(u, bmat, cmat, lam, lamseg, pw, d_skip.reshape(1, w), w_glu.astype(BF16), b_glu.reshape(1, w))
    return to_segments(out, lc // S5_SEGS, S5_SEGS)


def _rope_kernel(pos_ref, freq_ref, cos_ref, sin_ref):
    ang = pos_ref[...].astype(F32) * freq_ref[...]
    cos_ref[...] = jnp.cos(ang)
    sin_ref[...] = jnp.sin(ang)


def _rope_tables(positions, half, tm=2048):
    t = positions.size
    tm = min(tm, t)
    reps = LANES // half
    inv_freq = ROPE_BASE ** (-jnp.arange(half, dtype=F32) / half)
    freq = jnp.tile(inv_freq, reps).reshape(1, LANES)
    return pl.pallas_call(
        _rope_kernel,
        grid=(t // tm,),
        in_specs=[pl.BlockSpec((tm, 1), lambda i: (i, 0)), pl.BlockSpec((1, LANES), lambda i: (0, 0))],
        out_specs=[pl.BlockSpec((tm, LANES), lambda i: (i, 0))] * 2,
        out_shape=[jax.ShapeDtypeStruct((t, LANES), F32)] * 2,
        compiler_params=_params("parallel"),
        name="rope_tables",
    )(positions.reshape(t, 1), freq)


def _ret_kernel(q_ref, k_ref, v_ref, g_ref, cos_ref, sin_ref, dec_ref, zeta_ref, xi_ref, cd_ref,
                gnw_ref, o_ref, state_ref, *, dk, dv):
    @pl.when(pl.program_id(1) == 0)
    def _():
        state_ref[...] = jnp.zeros_like(state_ref)

    lt = q_ref.shape[0]
    half = dk // 2
    qkw = RET_HEADS * dk
    cos = jnp.concatenate([cos_ref[...]] * (qkw // LANES), axis=1)
    sin = jnp.concatenate([sin_ref[...]] * (qkw // LANES), axis=1)
    lane = lax.broadcasted_iota(jnp.int32, (lt, qkw), 1)
    first = (lane % dk) < half
    sin_signed = jnp.where(first, -sin, sin)

    def rot(x):
        swapped = jnp.where(first, pltpu.roll(x, qkw - half, 1), pltpu.roll(x, half, 1))
        return x * cos + swapped * sin_signed

    q = rot(q_ref[...].astype(F32))
    k = rot(k_ref[...].astype(F32)) * (dk ** -0.5)
    v = v_ref[...].astype(F32)
    g = g_ref[...].astype(F32)
    gnw = gnw_ref[...]

    for c0 in range(0, lt, RET_CHUNK):
        for h in range(RET_HEADS):
            qh = q[c0:c0 + RET_CHUNK, h * dk:(h + 1) * dk]
            kh = k[c0:c0 + RET_CHUNK, h * dk:(h + 1) * dk]
            vh = v[c0:c0 + RET_CHUNK, h * dv:(h + 1) * dv]
            st = state_ref[h]
            scores = lax.dot_general(qh, kh, (((1,), (1,)), ((), ())),
                                     preferred_element_type=F32) * dec_ref[h]
            o = jnp.dot(scores, vh, preferred_element_type=F32)
            o = o + jnp.dot(qh, st, preferred_element_type=F32) * xi_ref[h]
            kz = kh * zeta_ref[h]
            kv = lax.dot_general(kz, vh, (((0,), (0,)), ((), ())), preferred_element_type=F32)
            state_ref[h] = cd_ref[h] * st + kv
            mu = jnp.mean(o, axis=-1, keepdims=True)
            oc = o - mu
            var = jnp.mean(oc * oc, axis=-1, keepdims=True)
            on = oc * lax.rsqrt(var + EPS) * gnw[:, h * dv:(h + 1) * dv]
            gh = g[c0:c0 + RET_CHUNK, h * dv:(h + 1) * dv]
            o_ref[c0:c0 + RET_CHUNK, h * dv:(h + 1) * dv] = (gh * _sigmoid(gh) * on).astype(o_ref.dtype)


def _ret_consts(dk, dv):
    log_gamma = jnp.log1p(-(2.0 ** (-5.0 - jnp.arange(RET_HEADS, dtype=F32))))
    idx = jnp.arange(RET_CHUNK, dtype=F32)
    diff = idx[:, None] - idx[None, :]
    decay = jnp.where(diff >= 0, jnp.exp(log_gamma[:, None, None] * jnp.maximum(diff, 0.0)), 0.0)
    zeta = jnp.exp(log_gamma[:, None] * (RET_CHUNK - 1 - idx))[:, :, None]
    xi = jnp.exp(log_gamma[:, None] * (idx + 1.0))[:, :, None]
    cd = jnp.broadcast_to(jnp.exp(log_gamma * RET_CHUNK)[:, None, None], (RET_HEADS, dk, dv))
    return decay, jnp.broadcast_to(zeta, (RET_HEADS, RET_CHUNK, dk)), \
        jnp.broadcast_to(xi, (RET_HEADS, RET_CHUNK, dv)), cd


def _ret_mixer(proj, bsz, seq, cos_t, sin_t, gn_w, qk_w, v_w, q_col, lt=1024):
    dk = qk_w // RET_HEADS
    dv = v_w // RET_HEADS
    lt = min(lt, seq)
    nc = seq // lt
    decay, zeta, xi, cd = _ret_consts(dk, dv)
    qb = q_col // qk_w
    vb = (q_col + 2 * qk_w) // v_w
    row = lambda b, c: b * nc + c
    c3 = lambda b, c: (0, 0, 0)
    return pl.pallas_call(
        functools.partial(_ret_kernel, dk=dk, dv=dv),
        grid=(bsz, nc),
        in_specs=[
            pl.BlockSpec((lt, qk_w), lambda b, c: (row(b, c), qb)),
            pl.BlockSpec((lt, qk_w), lambda b, c: (row(b, c), qb + 1)),
            pl.BlockSpec((lt, v_w), lambda b, c: (row(b, c), vb)),
            pl.BlockSpec((lt, v_w), lambda b, c: (row(b, c), vb + 1)),
            pl.BlockSpec((lt, LANES), lambda b, c: (row(b, c), 0)),
            pl.BlockSpec((lt, LANES), lambda b, c: (row(b, c), 0)),
            pl.BlockSpec(decay.shape, c3),
            pl.BlockSpec(zeta.shape, c3),
            pl.BlockSpec(xi.shape, c3),
            pl.BlockSpec(cd.shape, c3),
            pl.BlockSpec((1, v_w), lambda b, c: (0, 0)),
        ],
        out_specs=pl.BlockSpec((lt, v_w), lambda b, c: (row(b, c), 0)),
        out_shape=jax.ShapeDtypeStruct((bsz * seq, v_w), ACT_DTYPE),
        scratch_shapes=[pltpu.VMEM((RET_HEADS, dk, dv), F32)],
        compiler_params=_params("arbitrary", "arbitrary"),
        name="retention_mixer",
    )(proj, proj, proj, proj, cos_t, sin_t, decay, zeta, xi, cd, gn_w.reshape(1, v_w))


def _lru_kernel(x_ref, gate_ref, cw_ref, cb_ref, wri_ref, bri_ref, sp_ref, o_ref,
                xe_ref, a_ref, b_ref, carry_ref):
    lc, w = x_ref.shape

    @pl.when(pl.program_id(1) == 0)
    def _():
        xe_ref[0:SUBLANES, :] = jnp.zeros((SUBLANES, w), F32)
        carry_ref[...] = jnp.zeros_like(carry_ref)

    x = x_ref[...].astype(F32)
    xe_ref[SUBLANES:, :] = x
    xc = cw_ref[CONV_WIDTH - 1:CONV_WIDTH, :] * x + cb_ref[...]
    for j in range(1, CONV_WIDTH):
        xc = xc + cw_ref[CONV_WIDTH - 1 - j:CONV_WIDTH - j, :] * xe_ref[pl.ds(SUBLANES - j, lc), :]
    xe_ref[0:SUBLANES, :] = x[lc - SUBLANES:, :]

    ri = _sigmoid(jnp.dot(xc.astype(BF16), wri_ref[...], preferred_element_type=F32) + bri_ref[...])
    r = ri[:, :w]
    gi = ri[:, w:]
    log_a = -LRU_C * r * sp_ref[...]
    a = jnp.exp(log_a)
    mult = jnp.sqrt(1.0 - a * a)
    b = mult * (gi * xc)

    rowmod = lax.broadcasted_iota(jnp.int32, (lc, w), 0) % SUBLANES
    for dist in (1, 2, 4):
        keep = rowmod >= dist
        a_sh = jnp.where(keep, pltpu.roll(a, dist, 0), 1.0)
        b_sh = jnp.where(keep, pltpu.roll(b, dist, 0), 0.0)
        b = a * b_sh + b
        a = a * a_sh
    a_ref[...] = a
    b_ref[...] = b

    def body(r8, hprev):
        rows = pl.ds(pl.multiple_of(r8 * SUBLANES, SUBLANES), SUBLANES)
        h = a_ref[rows, :] * hprev + b_ref[rows, :]
        b_ref[rows, :] = h
        return h[SUBLANES - 1:SUBLANES, :]

    carry_ref[...] = lax.fori_loop(0, lc // SUBLANES, body, carry_ref[...])
    o_ref[...] = (b_ref[...] * _gelu_gate(gate_ref[...].astype(F32))).astype(o_ref.dtype)


def _lru_mixer(proj, bsz, seq, conv_w, conv_b, w_r, b_r, w_i, b_i, lam, w, x_col, lc=1024):
    lc = min(lc, seq)
    nc = seq // lc
    eye = jnp.eye(LRU_BLOCKS, dtype=F32)
    dense = lambda m: jnp.einsum('ncd,nm->ncmd', m, eye).reshape(w, w)
    wri = jnp.concatenate([dense(w_r), dense(w_i)], axis=1).astype(BF16)
    bri = jnp.concatenate([b_r, b_i]).reshape(1, 2 * w)
    sp = jax.nn.softplus(-lam.astype(F32)).reshape(1, w)
    xb = x_col // w
    c2 = lambda b, c: (0, 0)
    return pl.pallas_call(
        _lru_kernel,
        grid=(bsz, nc),
        in_specs=[
            pl.BlockSpec((lc, w), lambda b, c: (b * nc + c, xb)),
            pl.BlockSpec((lc, w), lambda b, c: (b * nc + c, xb + 1)),
            pl.BlockSpec((CONV_WIDTH, w), c2),
            pl.BlockSpec((1, w), c2),
            pl.BlockSpec((w, 2 * w), c2),
            pl.BlockSpec((1, 2 * w), c2),
            pl.BlockSpec((1, w), c2),
        ],
        out_specs=pl.BlockSpec((lc, w), lambda b, c: (b * nc + c, 0)),
        out_shape=jax.ShapeDtypeStruct((bsz * seq, w), ACT_DTYPE),
        scratch_shapes=[pltpu.VMEM((lc + SUBLANES, w), F32), pltpu.VMEM((lc, w), F32),
                        pltpu.VMEM((lc, w), F32), pltpu.VMEM((1, w), F32)],
        compiler_params=_params("arbitrary", "arbitrary"),
        name="rglru_mixer",
    )(proj, proj, conv_w, conv_b.reshape(1, w), wri, bri, sp)


def _merge_kernel(y0_ref, y1_ref, y2_ref, gl_ref, x_ref, wb_ref, wo_ref, o_ref):
    d = x_ref.shape[1]
    merged = None
    for n, y_ref in enumerate((y0_ref, y1_ref, y2_ref)):
        br = jnp.dot(y_ref[...].astype(BF16), wb_ref[n], preferred_element_type=F32)
        term = _sigmoid(gl_ref[:, n * d:(n + 1) * d].astype(F32)) * br
        merged = term if merged is None else merged + term
    o_ref[...] = x_ref[...] + jnp.dot(merged.astype(BF16), wo_ref[...], preferred_element_type=F32)


def _merge(ys, proj, x2, w_branch, w_out, gate_col, tm=512):
    t, d = x2.shape
    bw = ys[0].shape[1]
    tm = min(tm, t)
    gb = gate_col // (N_BRANCH * d)
    yspec = pl.BlockSpec((tm, bw), lambda i: (i, 0))
    return pl.pallas_call(
        _merge_kernel,
        grid=(t // tm,),
        in_specs=[yspec, yspec, yspec,
                  pl.BlockSpec((tm, N_BRANCH * d), lambda i: (i, gb)),
                  pl.BlockSpec((tm, d), lambda i: (i, 0)),
                  pl.BlockSpec((N_BRANCH, bw, d), lambda i: (0, 0, 0)),
                  pl.BlockSpec((d, d), lambda i: (0, 0))],
        out_specs=pl.BlockSpec((tm, d), lambda i: (i, 0)),
        out_shape=jax.ShapeDtypeStruct((t, d), F32),
        compiler_params=_params("parallel"),
        name="merge_out",
    )(ys[0], ys[1], ys[2], proj, x2, w_branch.astype(BF16), w_out.astype(BF16))


_NEG = -1e30
PEER_SUB = 512
BF16_ROWS = 16
PEER_FLAGS = None


def _top_rows(s, count, n_ranked=0):
    rows = []
    rank = None
    if n_ranked:
        rank = jnp.full(s.shape, float(n_ranked), F32)
    for i in range(count):
        m = jnp.max(s, axis=0, keepdims=True)
        rows.append(m)
        if i + 1 < count or i < n_ranked:
            hit = s >= m
            if i < n_ranked:
                rank = jnp.where(hit, float(i), rank)
            s = jnp.where(hit, _NEG, s)
    return (rows, rank) if n_ranked else rows


def _batcher_pairs(lo, hi):
    def merge(lo, hi, r):
        step = r * 2
        if step < hi - lo:
            yield from merge(lo, hi, step)
            yield from merge(lo + r, hi, step)
            for i in range(lo + r, hi - r, step):
                yield (i, i + r)
        else:
            yield (lo, lo + r)

    if hi - lo >= 1:
        mid = lo + (hi - lo) // 2
        yield from _batcher_pairs(lo, mid)
        yield from _batcher_pairs(mid + 1, hi)
        yield from merge(lo, hi, 1)


def _top16_chunks(s):
    nrow = s.shape[0] // SUBLANES
    chunks = [s[r * SUBLANES:(r + 1) * SUBLANES, :] for r in range(nrow)]
    c = list(chunks)
    for i, j in _batcher_pairs(0, nrow - 1):
        c[i], c[j] = jnp.maximum(c[i], c[j]), jnp.minimum(c[i], c[j])
    for shift in (4, 2, 1):
        c = [jnp.maximum(c[r], pltpu.roll(c[nrow - 1 - r], shift, 0)) for r in range(nrow)]
        dist = nrow // 2
        while dist >= 1:
            for i in range(nrow):
                if not i & dist:
                    c[i], c[i + dist] = jnp.maximum(c[i], c[i + dist]), jnp.minimum(c[i], c[i + dist])
            dist //= 2
    rest = None
    for ch in chunks:
        m = jnp.where(ch < c[nrow - 1], ch, _NEG)
        rest = m if rest is None else jnp.maximum(rest, m)
    for shift in (4, 2, 1):
        rest = jnp.maximum(rest, pltpu.roll(rest, shift, 0))
    return chunks, c, rest


def _gelu_gate(x):
    k1 = -2.0 * _SQRT_2_OVER_PI * math.log2(math.e)
    k3 = k1 * 0.044715
    return x / (1.0 + jnp.exp2(x * (k1 + k3 * (x * x))))


def _peer_kernel(x_ref, nw_ref, wqt_ref, keys_ref, u_ref, vt_ref, fw_ref, o_ref,
                 xn_ref, qt_ref, rank_ref, e2_ref, cnt_ref, g_ref, acc_ref, *, final_norm):
    e = pl.program_id(1)
    ne = pl.num_programs(1)
    eb = u_ref.shape[0]
    tm = x_ref.shape[0]
    nk = PEER_NKEYS
    kk = PEER_TOPK + 1
    ntile = nk // BF16_ROWS

    @pl.when(e == 0)
    def _prep():
        xn = _rms(x_ref[...], nw_ref[...]).astype(BF16)
        xn_ref[...] = xn
        qt_ref[...] = lax.dot_general(wqt_ref[...], xn, (((1,), (1,)), ((), ())),
                                      preferred_element_type=F32)
        acc_ref[...] = jnp.zeros_like(acc_ref)

        def head(h, _):
            q1 = qt_ref[pl.ds(pl.multiple_of(h * 2 * PEER_HALF, PEER_HALF), PEER_HALF), :]
            q2 = qt_ref[pl.ds(pl.multiple_of(h * 2 * PEER_HALF + PEER_HALF, PEER_HALF), PEER_HALF), :]
            s1 = jnp.dot(keys_ref[2 * h], q1, preferred_element_type=F32)
            s2 = jnp.dot(keys_ref[2 * h + 1], q2, preferred_element_type=F32)
            ch1, t1, n1 = _top16_chunks(s1)
            ch2, t2, n2 = _top16_chunks(s2)
            v1 = [t[0:1, :] for t in t1] + [n1[0:1, :]]
            v2 = [t[0:1, :] for t in t2] + [n2[0:1, :]]
            cands = [v1[i] + v2[j] for i in range(kk) for j in range(kk) if (i + 1) * (j + 1) <= kk]
            cand = jnp.concatenate(cands, axis=0)
            top = _top_rows(cand, kk)
            theta = 0.5 * (top[PEER_TOPK - 1] + top[PEER_TOPK])
            cmax = v1[0] + v2[0]
            z = jnp.sum(jnp.where(cand >= theta, jnp.exp(cand - cmax), 0.0), axis=0, keepdims=True)
            theta8 = jnp.broadcast_to(theta, (SUBLANES, tm))
            zinv8 = jnp.broadcast_to(1.0 / z, (SUBLANES, tm))
            ranks, e2s = [], []
            for r, (c1, c2) in enumerate(zip(ch1, ch2)):
                rows = slice(r * SUBLANES, (r + 1) * SUBLANES)
                thr = theta8 - c1
                cnt = jnp.zeros_like(c1)
                rank = jnp.full(c2.shape, float(PEER_TOPK), F32)
                for j in range(PEER_TOPK):
                    cnt = jnp.where(t2[j] >= thr, float(j + 1), cnt)
                    jr = PEER_TOPK - 1 - j
                    rank = jnp.where(c2 >= t2[jr], float(jr), rank)
                cnt_ref[h, rows, :] = cnt
                g_ref[h, rows, :] = jnp.exp(c1 - t1[0]) * zinv8
                ranks.append(rank)
                e2s.append(jnp.exp(c2 - t2[0]))
            per = BF16_ROWS // SUBLANES
            for r in range(ntile):
                rank_ref[h, r] = jnp.concatenate(ranks[r * per:(r + 1) * per], axis=0).astype(BF16)
                e2_ref[h, r] = jnp.concatenate(e2s[r * per:(r + 1) * per], axis=0).astype(BF16)
            return 0

        lax.fori_loop(0, PEER_HEADS, head, 0)

    nsub = eb // PEER_SUB
    a_per = PEER_SUB // nk

    def gate(j):
        cnts, grs = {}, {}
        for ai in range(a_per):
            a = e * (eb // nk) + j * a_per + ai
            for h in range(PEER_HEADS):
                cnts[ai, h] = jnp.broadcast_to(cnt_ref[h, pl.ds(a, 1), :], (BF16_ROWS, tm)).astype(BF16)
                grs[ai, h] = jnp.broadcast_to(g_ref[h, pl.ds(a, 1), :], (BF16_ROWS, tm)).astype(BF16)
        wgts = [[None] * ntile for _ in range(a_per)]
        for r in range(ntile):
            for h in range(PEER_HEADS):
                rk = rank_ref[h, r]
                ee = e2_ref[h, r]
                for ai in range(a_per):
                    term = jnp.where(rk < cnts[ai, h], ee * grs[ai, h], 0.0)
                    wgts[ai][r] = term if h == 0 else wgts[ai][r] + term
        return wgts

    def score(j):
        return lax.dot_general(u_ref[j * PEER_SUB:(j + 1) * PEER_SUB, :], xn_ref[...],
                               (((1,), (1,)), ((), ())), preferred_element_type=F32)

    def hidden(act, wgts):
        parts = []
        for ai in range(a_per):
            gel = _gelu_gate(act[ai * nk:(ai + 1) * nk, :]).astype(BF16)
            for r in range(ntile):
                parts.append(gel[r * BF16_ROWS:(r + 1) * BF16_ROWS, :] * wgts[ai][r])
        return jnp.concatenate(parts, axis=0)

    hid = hidden(score(0), gate(0))
    for j in range(nsub):
        if j + 1 < nsub:
            wgts = gate(j + 1)
            act = score(j + 1)
        acc_ref[...] += jnp.dot(vt_ref[j], hid, preferred_element_type=F32)
        if j + 1 < nsub:
            hid = hidden(act, wgts)

    @pl.when(e == ne - 1)
    def _fin():
        out = x_ref[...] + acc_ref[...].T
        if final_norm:
            out = _rms(out, fw_ref[...])
        o_ref[...] = out


def _peer_values(v):
    n_exp, d = v.shape
    return v.astype(BF16).reshape(n_exp // PEER_SUB, PEER_SUB, d).transpose(0, 2, 1)


def _peer(x2, norm_w, w_query, sub_keys, u_bf16, vt_bf16, final_w, final_norm, tm=512, eb=2048):
    t, d = x2.shape
    n_exp = u_bf16.shape[0]
    tm = min(tm, t)
    wqt = w_query.T.astype(BF16)
    keys = sub_keys.reshape(PEER_HEADS * 2, PEER_NKEYS, PEER_HALF).astype(F32)
    c2 = lambda i, e: (0, 0)
    head_buf = pltpu.VMEM((PEER_HEADS, PEER_NKEYS, tm), F32)
    packed_buf = pltpu.VMEM((PEER_HEADS, PEER_NKEYS // BF16_ROWS, BF16_ROWS, tm), BF16)
    return pl.pallas_call(
        functools.partial(_peer_kernel, final_norm=final_norm),
        grid=(t // tm, n_exp // eb),
        in_specs=[
            pl.BlockSpec((tm, d), lambda i, e: (i, 0)),
            pl.BlockSpec((1, d), c2),
            pl.BlockSpec(wqt.shape, c2),
            pl.BlockSpec(keys.shape, lambda i, e: (0, 0, 0)),
            pl.BlockSpec((eb, d), lambda i, e: (e, 0)),
            pl.BlockSpec((eb // PEER_SUB, d, PEER_SUB), lambda i, e: (e, 0, 0)),
            pl.BlockSpec((1, d), c2),
        ],
        out_specs=pl.BlockSpec((tm, d), lambda i, e: (i, 0)),
        out_shape=jax.ShapeDtypeStruct((t, d), F32),
        scratch_shapes=[pltpu.VMEM((tm, d), BF16), pltpu.VMEM((wqt.shape[0], tm), F32),
                        packed_buf, packed_buf, head_buf, head_buf, pltpu.VMEM((d, tm), F32)],
        compiler_params=_params("parallel", "arbitrary", flags=PEER_FLAGS),
        name="peer_dense",
    )(x2, norm_w.reshape(1, d), wqt, keys, u_bf16, vt_bf16, final_w.reshape(1, d))


def kernel(x, positions, norm_mix, w_in, s5_lam_re, s5_lam_im, s5_log_dt, s5_b_re, s5_b_im, s5_c_re, s5_c_im, s5_d, s5_w_glu, s5_b_glu, ret_gn_w, lru_conv_w, lru_conv_b, lru_w_r, lru_b_r, lru_w_i, lru_b_i, lru_lam, w_branch, w_out, norm_ffn, peer_w_query, peer_sub_keys, peer_u, peer_v, final_norm):
    bsz, seq, d = x.shape
    depth = norm_mix.shape[0]
    s5_w = s5_d.shape[1]
    v_w = ret_gn_w.shape[1]
    qk_w = v_w // 2
    lru_w = lru_lam.shape[1]
    q_col = s5_w
    x_col = q_col + 2 * qk_w + 2 * v_w
    gate_col = x_col + 2 * lru_w

    x2 = x.reshape(bsz * seq, d)
    cos_t, sin_t = _rope_tables(positions, qk_w // RET_HEADS // 2)
    for l in range(depth):
        proj = _norm_proj(x2, norm_mix[l], w_in[l].astype(BF16))
        s5c = _s5_consts(s5_lam_re[l], s5_lam_im[l], s5_log_dt[l], s5_b_re[l], s5_b_im[l],
                         s5_c_re[l], s5_c_im[l], min(S5_CHUNK, seq) // S5_SEGS)
        y_s5 = _s5_mixer(proj, bsz, seq, s5c, s5_d[l], s5_w_glu[l], s5_b_glu[l])
        y_ret = _ret_mixer(proj, bsz, seq, cos_t, sin_t, ret_gn_w[l], qk_w, v_w, q_col)
        y_lru = _lru_mixer(proj, bsz, seq, lru_conv_w[l], lru_conv_b[l], lru_w_r[l], lru_b_r[l],
                           lru_w_i[l], lru_b_i[l], lru_lam[l], lru_w, x_col)
        x2 = _merge((y_s5, y_ret, y_lru), proj, x2, w_branch[l], w_out[l], gate_col)
        x2 = _peer(x2, norm_ffn[l], peer_w_query[l], peer_sub_keys[l], peer_u[l].astype(BF16),
                   _peer_values(peer_v[l]), final_norm, final_norm=(l == depth - 1))
    return x2.reshape(bsz, seq, d)
```

```python
import functools
import math

import jax
import jax.numpy as jnp
import numpy as np
from jax import lax
from jax.experimental import pallas as pl
from jax.experimental.pallas import tpu as pltpu

F32 = jnp.float32
BF16 = jnp.bfloat16
ACT_DTYPE = BF16

EPS = 1e-6
S5_GROUP = 16
S5_STATE = 64
RET_HEADS = 4
RET_CHUNK = 128
ROPE_BASE = 10000.0
LRU_BLOCKS = 8
CONV_WIDTH = 4
LRU_C = 8.0
N_BRANCH = 3
PEER_HEADS = 8
PEER_NKEYS = 128
PEER_TOPK = 16
PEER_HALF = 64

SUBLANES = 8
LANES = 128
VMEM_LIMIT = 56 * 1024 * 1024

_SQRT_2_OVER_PI = math.sqrt(2.0 / math.pi)


def _gelu(x):
    return 0.5 * x * (1.0 + jnp.tanh(_SQRT_2_OVER_PI * (x + 0.044715 * (x * x * x))))


def _sigmoid(x):
    return 1.0 / (1.0 + jnp.exp(-x))


def _rms(x, w):
    return x * lax.rsqrt(jnp.mean(x * x, axis=-1, keepdims=True) + EPS) * w


def _params(*sem, flags=None):
    return pltpu.CompilerParams(dimension_semantics=sem, vmem_limit_bytes=VMEM_LIMIT, flags=flags)


def _norm_proj_kernel(x_ref, nw_ref, w_ref, o_ref, xn_ref):
    @pl.when(pl.program_id(1) == 0)
    def _():
        xn_ref[...] = _rms(x_ref[...], nw_ref[...]).astype(BF16)

    o_ref[...] = jnp.dot(xn_ref[...], w_ref[...], preferred_element_type=F32).astype(o_ref.dtype)


def _norm_proj(x2, norm_w, w_bf16, tm=1024, tn=3072):
    t, d = x2.shape
    n = w_bf16.shape[1]
    tm = min(tm, t)
    return pl.pallas_call(
        _norm_proj_kernel,
        grid=(t // tm, n // tn),
        in_specs=[
            pl.BlockSpec((tm, d), lambda i, j: (i, 0)),
            pl.BlockSpec((1, d), lambda i, j: (0, 0)),
            pl.BlockSpec((d, tn), lambda i, j: (0, j)),
        ],
        out_specs=pl.BlockSpec((tm, tn), lambda i, j: (i, j)),
        out_shape=jax.ShapeDtypeStruct((t, n), ACT_DTYPE),
        scratch_shapes=[pltpu.VMEM((tm, d), BF16)],
        compiler_params=_params("parallel", "arbitrary"),
        name="norm_proj",
    )(x2, norm_w.reshape(1, d), w_bf16)


S5_LANE_CHUNK = 512
S5_SPLIT = 2
S5_CHUNK = 512
S5_SEGS = SUBLANES
S5_UNROLL = 4


def _s5_kernel(u_ref, bmat_ref, cmat_ref, lam_ref, lamseg_ref, pw_ref, d_ref, wglu_ref, bglu_ref, o_ref,
               bu_ref, carry_ref, *, ns):
    nh = ns // S5_SPLIT

    @pl.when(pl.program_id(1) == 0)
    def _():
        carry_ref[...] = jnp.zeros_like(carry_ref)

    u = u_ref[...].astype(F32)
    ub = u.astype(BF16)
    lc, w = u.shape
    seg_len = lc // S5_SEGS
    for hh in range(S5_SPLIT):
        part = jnp.dot(ub[:, hh * (w // S5_SPLIT):(hh + 1) * (w // S5_SPLIT)], bmat_ref[hh],
                       preferred_element_type=F32)
        bu_ref[:, hh * nh:(hh + 1) * nh] = part[:, :nh]
        bu_ref[:, ns + hh * nh:ns + (hh + 1) * nh] = part[:, nh:]

    for c0 in range(0, ns, S5_LANE_CHUNK):
        cols = slice(c0, c0 + S5_LANE_CHUNK)
        re_sl = pl.ds(c0, S5_LANE_CHUNK)
        im_sl = pl.ds(ns + c0, S5_LANE_CHUNK)
        ar, ai = lam_ref[0, :, cols], lam_ref[1, :, cols]

        def local(t, h, re_sl=re_sl, im_sl=im_sl, ar=ar, ai=ai):
            hr, hi = h
            rows = pl.ds(pl.multiple_of(t * S5_SEGS, S5_SEGS), S5_SEGS)
            nr = ar * hr - ai * hi + bu_ref[rows, re_sl]
            ni = ar * hi + ai * hr + bu_ref[rows, im_sl]
            bu_ref[rows, re_sl] = nr
            bu_ref[rows, im_sl] = ni
            return nr, ni

        zero = jnp.zeros((S5_SEGS, S5_LANE_CHUNK), F32)
        er, ei = lax.fori_loop(0, seg_len, local, (zero, zero), unroll=S5_UNROLL)

        sr, si = lamseg_ref[0, :, cols], lamseg_ref[1, :, cols]
        hin_r, hin_i = [carry_ref[:, re_sl]], [carry_ref[:, im_sl]]
        for k in range(S5_SEGS):
            pr, pim = hin_r[k], hin_i[k]
            hin_r.append(er[k:k + 1, :] + sr * pr - si * pim)
            hin_i.append(ei[k:k + 1, :] + sr * pim + si * pr)
        carry_ref[:, re_sl] = hin_r[S5_SEGS]
        carry_ref[:, im_sl] = hin_i[S5_SEGS]
        hr_in = jnp.concatenate(hin_r[:S5_SEGS], axis=0)
        hi_in = jnp.concatenate(hin_i[:S5_SEGS], axis=0)

        def fix(t, carry, re_sl=re_sl, im_sl=im_sl, hr_in=hr_in, hi_in=hi_in):
            rows = pl.ds(pl.multiple_of(t * S5_SEGS, S5_SEGS), S5_SEGS)
            pr = pw_ref[rows, re_sl]
            pim = pw_ref[rows, im_sl]
            bu_ref[rows, re_sl] = bu_ref[rows, re_sl] + pr * hr_in - pim * hi_in
            bu_ref[rows, im_sl] = bu_ref[rows, im_sl] + pr * hi_in + pim * hr_in
            return carry

        lax.fori_loop(0, seg_len, fix, 0, unroll=S5_UNROLL)

    ys = []
    for hh in range(S5_SPLIT):
        hs = jnp.concatenate([bu_ref[:, hh * nh:(hh + 1) * nh],
                              bu_ref[:, ns + hh * nh:ns + (hh + 1) * nh]], axis=1)
        ys.append(jnp.dot(hs.astype(BF16), cmat_ref[hh], preferred_element_type=F32))
    y = jnp.concatenate(ys, axis=1)
    z = _gelu_gate(y + d_ref[...] * u)
    gl = jnp.dot(z.astype(BF16), wglu_ref[...], preferred_element_type=F32) + bglu_ref[...]
    o_ref[...] = (z * _sigmoid(gl)).astype(o_ref.dtype)


def _s5_consts(lam_re, lam_im, log_dt, b_re, b_im, c_re, c_im, seg_len):
    g, p = lam_re.shape
    h16 = b_re.shape[-1]
    lr, li = lam_re.astype(F32), lam_im.astype(F32)
    dt = jnp.broadcast_to(jnp.exp(log_dt.astype(F32))[:, None], lr.shape)

    def lam_bar_pow(lr, li, dt, n):
        mag = jnp.exp(lr * dt * n)
        return mag * jnp.cos(li * dt * n), mag * jnp.sin(li * dt * n)

    br_, bi_ = lam_bar_pow(lr, li, dt, 1.0)
    den = lr * lr + li * li
    cr = ((br_ - 1.0) * lr + bi_ * li) / den
    ci = (bi_ * lr - (br_ - 1.0) * li) / den
    bre, bim = b_re.astype(F32), b_im.astype(F32)
    bbar_re = cr[..., None] * bre - ci[..., None] * bim
    bbar_im = cr[..., None] * bim + ci[..., None] * bre
    gs = g // S5_SPLIT
    eye = jnp.eye(gs, dtype=F32)

    def bdiag_in(m):
        return jnp.einsum('gph,gk->ghkp', m, eye).reshape(gs * h16, gs * p)

    def bdiag_out(m):
        return jnp.einsum('ghp,gk->gpkh', m, eye).reshape(gs * p, gs * h16)

    halves = lambda m: [m[hh * gs:(hh + 1) * gs] for hh in range(S5_SPLIT)]
    bmat = jnp.stack([jnp.concatenate([bdiag_in(r), bdiag_in(i)], axis=1)
                      for r, i in zip(halves(bbar_re), halves(bbar_im))])
    cmat = jnp.stack([jnp.concatenate([bdiag_out(r), bdiag_out(i)], axis=0)
                      for r, i in zip(halves(c_re.astype(F32)), halves(-c_im.astype(F32)))])

    flat = lambda m: m.reshape(1, g * p)
    lam1 = lam_bar_pow(flat(lr), flat(li), flat(dt), 1.0)
    lam = jnp.stack([jnp.broadcast_to(m, (S5_SEGS, g * p)) for m in lam1])
    lamseg = jnp.stack(lam_bar_pow(flat(lr), flat(li), flat(dt), float(seg_len)))
    steps = jnp.arange(1, seg_len + 1, dtype=F32)[:, None]
    pw = jnp.concatenate(lam_bar_pow(flat(lr), flat(li), flat(dt), steps), axis=1)
    pw = jnp.repeat(pw, S5_SEGS, axis=0)
    return bmat.astype(BF16), cmat.astype(BF16), lam.astype(F32), lamseg.astype(F32), pw.astype(F32)


def _s5_mixer(proj, bsz, seq, consts, d_skip, w_glu, b_glu):
    bmat, cmat, lam, lamseg, pw = consts
    w = bmat.shape[0] * bmat.shape[1]
    ns = bmat.shape[0] * bmat.shape[2] // 2
    lc = pw.shape[0]
    nc = seq // lc

    def to_segments(m, rows_a, rows_b):
        return m.reshape(bsz, nc, rows_a, rows_b, w).transpose(0, 1, 3, 2, 4).reshape(bsz * seq, w)

    u = to_segments(proj[:, :w], S5_SEGS, lc // S5_SEGS)
    const2 = lambda b, c: (0, 0)
    const3 = lambda b, c: (0, 0, 0)
    out = pl.pallas_call(
        functools.partial(_s5_kernel, ns=ns),
        grid=(bsz, nc),
        in_specs=[
            pl.BlockSpec((lc, w), lambda b, c: (b * nc + c, 0)),
            pl.BlockSpec(bmat.shape, const3),
            pl.BlockSpec(cmat.shape, const3),
            pl.BlockSpec(lam.shape, const3),
            pl.BlockSpec(lamseg.shape, const3),
            pl.BlockSpec(pw.shape, const2),
            pl.BlockSpec((1, w), const2),
            pl.BlockSpec((w, w), const2),
            pl.BlockSpec((1, w), const2),
        ],
        out_specs=pl.BlockSpec((lc, w), lambda b, c: (b * nc + c, 0)),
        out_shape=jax.ShapeDtypeStruct((bsz * seq, w), ACT_DTYPE),
        scratch_shapes=[pltpu.VMEM((lc, 2 * ns), F32), pltpu.VMEM((1, 2 * ns), F32)],
        compiler_params=_params("arbitrary", "arbitrary"),
        name="s5_mixer",
    )(u, bmat, cmat, lam, lamseg, pw, d_skip.reshape(1, w), w_glu.astype(BF16), b_glu.reshape(1, w))
    return to_segments(out, lc // S5_SEGS, S5_SEGS)


def _rope_kernel(pos_ref, freq_ref, cos_ref, sin_ref):
    ang = pos_ref[...].astype(F32) * freq_ref[...]
    cos_ref[...] = jnp.cos(ang)
    sin_ref[...] = jnp.sin(ang)


def _rope_tables(positions, half, tm=2048):
    t = positions.size
    tm = min(tm, t)
    reps = LANES // half
    inv_freq = ROPE_BASE ** (-jnp.arange(half, dtype=F32) / half)
    freq = jnp.tile(inv_freq, reps).reshape(1, LANES)
    return pl.pallas_call(
        _rope_kernel,
        grid=(t // tm,),
        in_specs=[pl.BlockSpec((tm, 1), lambda i: (i, 0)), pl.BlockSpec((1, LANES), lambda i: (0, 0))],
        out_specs=[pl.BlockSpec((tm, LANES), lambda i: (i, 0))] * 2,
        out_shape=[jax.ShapeDtypeStruct((t, LANES), F32)] * 2,
        compiler_params=_params("parallel"),
        name="rope_tables",
    )(positions.reshape(t, 1), freq)


def _ret_kernel(q_ref, k_ref, v_ref, g_ref, cos_ref, sin_ref, dec_ref, zeta_ref, xi_ref, cd_ref,
                gnw_ref, o_ref, state_ref, *, dk, dv):
    @pl.when(pl.program_id(1) == 0)
    def _():
        state_ref[...] = jnp.zeros_like(state_ref)

    lt = q_ref.shape[0]
    half = dk // 2
    qkw = RET_HEADS * dk
    cos = jnp.concatenate([cos_ref[...]] * (qkw // LANES), axis=1)
    sin = jnp.concatenate([sin_ref[...]] * (qkw // LANES), axis=1)
    lane = lax.broadcasted_iota(jnp.int32, (lt, qkw), 1)
    first = (lane % dk) < half
    sin_signed = jnp.where(first, -sin, sin)

    def rot(x):
        swapped = jnp.where(first, pltpu.roll(x, qkw - half, 1), pltpu.roll(x, half, 1))
        return x * cos + swapped * sin_signed

    q = rot(q_ref[...].astype(F32))
    k = rot(k_ref[...].astype(F32)) * (dk ** -0.5)
    v = v_ref[...].astype(F32)
    g = g_ref[...].astype(F32)
    gnw = gnw_ref[...]

    for c0 in range(0, lt, RET_CHUNK):
        for h in range(RET_HEADS):
            qh = q[c0:c0 + RET_CHUNK, h * dk:(h + 1) * dk]
            kh = k[c0:c0 + RET_CHUNK, h * dk:(h + 1) * dk]
            vh = v[c0:c0 + RET_CHUNK, h * dv:(h + 1) * dv]
            st = state_ref[h]
            scores = lax.dot_general(qh, kh, (((1,), (1,)), ((), ())),
                                     preferred_element_type=F32) * dec_ref[h]
            o = jnp.dot(scores, vh, preferred_element_type=F32)
            o = o + jnp.dot(qh, st, preferred_element_type=F32) * xi_ref[h]
            kz = kh * zeta_ref[h]
            kv = lax.dot_general(kz, vh, (((0,), (0,)), ((), ())), preferred_element_type=F32)
            state_ref[h] = cd_ref[h] * st + kv
            mu = jnp.mean(o, axis=-1, keepdims=True)
            oc = o - mu
            var = jnp.mean(oc * oc, axis=-1, keepdims=True)
            on = oc * lax.rsqrt(var + EPS) * gnw[:, h * dv:(h + 1) * dv]
            gh = g[c0:c0 + RET_CHUNK, h * dv:(h + 1) * dv]
            o_ref[c0:c0 + RET_CHUNK, h * dv:(h + 1) * dv] = (gh * _sigmoid(gh) * on).astype(o_ref.dtype)


def _ret_consts(dk, dv):
    log_gamma = jnp.log1p(-(2.0 ** (-5.0 - jnp.arange(RET_HEADS, dtype=F32))))
    idx = jnp.arange(RET_CHUNK, dtype=F32)
    diff = idx[:, None] - idx[None, :]
    decay = jnp.where(diff >= 0, jnp.exp(log_gamma[:, None, None] * jnp.maximum(diff, 0.0)), 0.0)
    zeta = jnp.exp(log_gamma[:, None] * (RET_CHUNK - 1 - idx))[:, :, None]
    xi = jnp.exp(log_gamma[:, None] * (idx + 1.0))[:, :, None]
    cd = jnp.broadcast_to(jnp.exp(log_gamma * RET_CHUNK)[:, None, None], (RET_HEADS, dk, dv))
    return decay, jnp.broadcast_to(zeta, (RET_HEADS, RET_CHUNK, dk)), \
        jnp.broadcast_to(xi, (RET_HEADS, RET_CHUNK, dv)), cd


def _ret_mixer(proj, bsz, seq, cos_t, sin_t, gn_w, qk_w, v_w, q_col, lt=1024):
    dk = qk_w // RET_HEADS
    dv = v_w // RET_HEADS
    lt = min(lt, seq)
    nc = seq // lt
    decay, zeta, xi, cd = _ret_consts(dk, dv)
    qb = q_col // qk_w
    vb = (q_col + 2 * qk_w) // v_w
    row = lambda b, c: b * nc + c
    c3 = lambda b, c: (0, 0, 0)
    return pl.pallas_call(
        functools.partial(_ret_kernel, dk=dk, dv=dv),
        grid=(bsz, nc),
        in_specs=[
            pl.BlockSpec((lt, qk_w), lambda b, c: (row(b, c), qb)),
            pl.BlockSpec((lt, qk_w), lambda b, c: (row(b, c), qb + 1)),
            pl.BlockSpec((lt, v_w), lambda b, c: (row(b, c), vb)),
            pl.BlockSpec((lt, v_w), lambda b, c: (row(b, c), vb + 1)),
            pl.BlockSpec((lt, LANES), lambda b, c: (row(b, c), 0)),
            pl.BlockSpec((lt, LANES), lambda b, c: (row(b, c), 0)),
            pl.BlockSpec(decay.shape, c3),
            pl.BlockSpec(zeta.shape, c3),
            pl.BlockSpec(xi.shape, c3),
            pl.BlockSpec(cd.shape, c3),
            pl.BlockSpec((1, v_w), lambda b, c: (0, 0)),
        ],
        out_specs=pl.BlockSpec((lt, v_w), lambda b, c: (row(b, c), 0)),
        out_shape=jax.ShapeDtypeStruct((bsz * seq, v_w), ACT_DTYPE),
        scratch_shapes=[pltpu.VMEM((RET_HEADS, dk, dv), F32)],
        compiler_params=_params("arbitrary", "arbitrary"),
        name="retention_mixer",
    )(proj, proj, proj, proj, cos_t, sin_t, decay, zeta, xi, cd, gn_w.reshape(1, v_w))


def _lru_kernel(x_ref, gate_ref, cw_ref, cb_ref, wri_ref, bri_ref, sp_ref, o_ref,
                xe_ref, a_ref, b_ref, carry_ref):
    lc, w = x_ref.shape

    @pl.when(pl.program_id(1) == 0)
    def _():
        xe_ref[0:SUBLANES, :] = jnp.zeros((SUBLANES, w), F32)
        carry_ref[...] = jnp.zeros_like(carry_ref)

    x = x_ref[...].astype(F32)
    xe_ref[SUBLANES:, :] = x
    xc = cw_ref[CONV_WIDTH - 1:CONV_WIDTH, :] * x + cb_ref[...]
    for j in range(1, CONV_WIDTH):
        xc = xc + cw_ref[CONV_WIDTH - 1 - j:CONV_WIDTH - j, :] * xe_ref[pl.ds(SUBLANES - j, lc), :]
    xe_ref[0:SUBLANES, :] = x[lc - SUBLANES:, :]

    ri = _sigmoid(jnp.dot(xc.astype(BF16), wri_ref[...], preferred_element_type=F32) + bri_ref[...])
    r = ri[:, :w]
    gi = ri[:, w:]
    log_a = -LRU_C * r * sp_ref[...]
    a = jnp.exp(log_a)
    mult = jnp.sqrt(1.0 - a * a)
    b = mult * (gi * xc)

    rowmod = lax.broadcasted_iota(jnp.int32, (lc, w), 0) % SUBLANES
    for dist in (1, 2, 4):
        keep = rowmod >= dist
        a_sh = jnp.where(keep, pltpu.roll(a, dist, 0), 1.0)
        b_sh = jnp.where(keep, pltpu.roll(b, dist, 0), 0.0)
        b = a * b_sh + b
        a = a * a_sh
    a_ref[...] = a
    b_ref[...] = b

    def body(r8, hprev):
        rows = pl.ds(pl.multiple_of(r8 * SUBLANES, SUBLANES), SUBLANES)
        h = a_ref[rows, :] * hprev + b_ref[rows, :]
        b_ref[rows, :] = h
        return h[SUBLANES - 1:SUBLANES, :]

    carry_ref[...] = lax.fori_loop(0, lc // SUBLANES, body, carry_ref[...])
    o_ref[...] = (b_ref[...] * _gelu_gate(gate_ref[...].astype(F32))).astype(o_ref.dtype)


def _lru_mixer(proj, bsz, seq, conv_w, conv_b, w_r, b_r, w_i, b_i, lam, w, x_col, lc=1024):
    lc = min(lc, seq)
    nc = seq // lc
    eye = jnp.eye(LRU_BLOCKS, dtype=F32)
    dense = lambda m: jnp.einsum('ncd,nm->ncmd', m, eye).reshape(w, w)
    wri = jnp.concatenate([dense(w_r), dense(w_i)], axis=1).astype(BF16)
    bri = jnp.concatenate([b_r, b_i]).reshape(1, 2 * w)
    sp = jax.nn.softplus(-lam.astype(F32)).reshape(1, w)
    xb = x_col // w
    c2 = lambda b, c: (0, 0)
    return pl.pallas_call(
        _lru_kernel,
        grid=(bsz, nc),
        in_specs=[
            pl.BlockSpec((lc, w), lambda b, c: (b * nc + c, xb)),
            pl.BlockSpec((lc, w), lambda b, c: (b * nc + c, xb + 1)),
            pl.BlockSpec((CONV_WIDTH, w), c2),
            pl.BlockSpec((1, w), c2),
            pl.BlockSpec((w, 2 * w), c2),
            pl.BlockSpec((1, 2 * w), c2),
            pl.BlockSpec((1, w), c2),
        ],
        out_specs=pl.BlockSpec((lc, w), lambda b, c: (b * nc + c, 0)),
        out_shape=jax.ShapeDtypeStruct((bsz * seq, w), ACT_DTYPE),
        scratch_shapes=[pltpu.VMEM((lc + SUBLANES, w), F32), pltpu.VMEM((lc, w), F32),
                        pltpu.VMEM((lc, w), F32), pltpu.VMEM((1, w), F32)],
        compiler_params=_params("arbitrary", "arbitrary"),
        name="rglru_mixer",
    )(proj, proj, conv_w, conv_b.reshape(1, w), wri, bri, sp)


def _merge_kernel(y0_ref, y1_ref, y2_ref, gl_ref, x_ref, wb_ref, wo_ref, o_ref):
    d = x_ref.shape[1]
    merged = None
    for n, y_ref in enumerate((y0_ref, y1_ref, y2_ref)):
        br = jnp.dot(y_ref[...].astype(BF16), wb_ref[n], preferred_element_type=F32)
        term = _sigmoid(gl_ref[:, n * d:(n + 1) * d].astype(F32)) * br
        merged = term if merged is None else merged + term
    o_ref[...] = x_ref[...] + jnp.dot(merged.astype(BF16), wo_ref[...], preferred_element_type=F32)


def _merge(ys, proj, x2, w_branch, w_out, gate_col, tm=512):
    t, d = x2.shape
    bw = ys[0].shape[1]
    tm = min(tm, t)
    gb = gate_col // (N_BRANCH * d)
    yspec = pl.BlockSpec((tm, bw), lambda i: (i, 0))
    return pl.pallas_call(
        _merge_kernel,
        grid=(t // tm,),
        in_specs=[yspec, yspec, yspec,
                  pl.BlockSpec((tm, N_BRANCH * d), lambda i: (i, gb)),
                  pl.BlockSpec((tm, d), lambda i: (i, 0)),
                  pl.BlockSpec((N_BRANCH, bw, d), lambda i: (0, 0, 0)),
                  pl.BlockSpec((d, d), lambda i: (0, 0))],
        out_specs=pl.BlockSpec((tm, d), lambda i: (i, 0)),
        out_shape=jax.ShapeDtypeStruct((t, d), F32),
        compiler_params=_params("parallel"),
        name="merge_out",
    )(ys[0], ys[1], ys[2], proj, x2, w_branch.astype(BF16), w_out.astype(BF16))


_NEG = -1e30
PEER_SUB = 512
BF16_ROWS = 16
PEER_FLAGS = None


def _top_rows(s, count, n_ranked=0):
    rows = []
    rank = None
    if n_ranked:
        rank = jnp.full(s.shape, float(n_ranked), F32)
    for i in range(count):
        m = jnp.max(s, axis=0, keepdims=True)
        rows.append(m)
        if i + 1 < count or i < n_ranked:
            hit = s >= m
            if i < n_ranked:
                rank = jnp.where(hit, float(i), rank)
            s = jnp.where(hit, _NEG, s)
    return (rows, rank) if n_ranked else rows


def _batcher_pairs(lo, hi):
    def merge(lo, hi, r):
        step = r * 2
        if step < hi - lo:
            yield from merge(lo, hi, step)
            yield from merge(lo + r, hi, step)
            for i in range(lo + r, hi - r, step):
                yield (i, i + r)
        else:
            yield (lo, lo + r)

    if hi - lo >= 1:
        mid = lo + (hi - lo) // 2
        yield from _batcher_pairs(lo, mid)
        yield from _batcher_pairs(mid + 1, hi)
        yield from merge(lo, hi, 1)


def _top16_chunks(s):
    nrow = s.shape[0] // SUBLANES
    chunks = [s[r * SUBLANES:(r + 1) * SUBLANES, :] for r in range(nrow)]
    c = list(chunks)
    for i, j in _batcher_pairs(0, nrow - 1):
        c[i], c[j] = jnp.maximum(c[i], c[j]), jnp.minimum(c[i], c[j])
    for shift in (4, 2, 1):
        c = [jnp.maximum(c[r], pltpu.roll(c[nrow - 1 - r], shift, 0)) for r in range(nrow)]
        dist = nrow // 2
        while dist >= 1:
            for i in range(nrow):
                if not i & dist:
                    c[i], c[i + dist] = jnp.maximum(c[i], c[i + dist]), jnp.minimum(c[i], c[i + dist])
            dist //= 2
    rest = None
    for ch in chunks:
        m = jnp.where(ch < c[nrow - 1], ch, _NEG)
        rest = m if rest is None else jnp.maximum(rest, m)
    for shift in (4, 2, 1):
        rest = jnp.maximum(rest, pltpu.roll(rest, shift, 0))
    return chunks, c, rest


def _gelu_gate(x):
    k1 = -2.0 * _SQRT_2_OVER_PI * math.log2(math.e)
    k3 = k1 * 0.044715
    return x / (1.0 + jnp.exp2(x * (k1 + k3 * (x * x))))


def _peer_kernel(x_ref, nw_ref, wqt_ref, keys_ref, u_ref, vt_ref, fw_ref, o_ref,
                 xn_ref, qt_ref, rank_ref, e2_ref, cnt_ref, g_ref, acc_ref, *, final_norm):
    e = pl.program_id(1)
    ne = pl.num_programs(1)
    eb = u_ref.shape[0]
    tm = x_ref.shape[0]
    nk = PEER_NKEYS
    kk = PEER_TOPK + 1
    ntile = nk // BF16_ROWS

    @pl.when(e == 0)
    def _prep():
        xn = _rms(x_ref[...], nw_ref[...]).astype(BF16)
        xn_ref[...] = xn
        qt_ref[...] = lax.dot_general(wqt_ref[...], xn, (((1,), (1,)), ((), ())),
                                      preferred_element_type=F32)
        acc_ref[...] = jnp.zeros_like(acc_ref)

        def head(h, _):
            q1 = qt_ref[pl.ds(pl.multiple_of(h * 2 * PEER_HALF, PEER_HALF), PEER_HALF), :]
            q2 = qt_ref[pl.ds(pl.multiple_of(h * 2 * PEER_HALF + PEER_HALF, PEER_HALF), PEER_HALF), :]
            s1 = jnp.dot(keys_ref[2 * h], q1, preferred_element_type=F32)
            s2 = jnp.dot(keys_ref[2 * h + 1], q2, preferred_element_type=F32)
            ch1, t1, n1 = _top16_chunks(s1)
            ch2, t2, n2 = _top16_chunks(s2)
            v1 = [t[0:1, :] for t in t1] + [n1[0:1, :]]
            v2 = [t[0:1, :] for t in t2] + [n2[0:1, :]]
            cands = [v1[i] + v2[j] for i in range(kk) for j in range(kk) if (i + 1) * (j + 1) <= kk]
            cand = jnp.concatenate(cands, axis=0)
            top = _top_rows(cand, kk)
            theta = 0.5 * (top[PEER_TOPK - 1] + top[PEER_TOPK])
            cmax = v1[0] + v2[0]
            z = jnp.sum(jnp.where(cand >= theta, jnp.exp(cand - cmax), 0.0), axis=0, keepdims=True)
            theta8 = jnp.broadcast_to(theta, (SUBLANES, tm))
            zinv8 = jnp.broadcast_to(1.0 / z, (SUBLANES, tm))
            ranks, e2s = [], []
            for r, (c1, c2) in enumerate(zip(ch1, ch2)):
                rows = slice(r * SUBLANES, (r + 1) * SUBLANES)
                thr = theta8 - c1
                cnt = jnp.zeros_like(c1)
                rank = jnp.full(c2.shape, float(PEER_TOPK), F32)
                for j in range(PEER_TOPK):
                    cnt = jnp.where(t2[j] >= thr, float(j + 1), cnt)
                    jr = PEER_TOPK - 1 - j
                    rank = jnp.where(c2 >= t2[jr], float(jr), rank)
                cnt_ref[h, rows, :] = cnt
                g_ref[h, rows, :] = jnp.exp(c1 - t1[0]) * zinv8
                ranks.append(rank)
                e2s.append(jnp.exp(c2 - t2[0]))
            per = BF16_ROWS // SUBLANES
            for r in range(ntile):
                rank_ref[h, r] = jnp.concatenate(ranks[r * per:(r + 1) * per], axis=0).astype(BF16)
                e2_ref[h, r] = jnp.concatenate(e2s[r * per:(r + 1) * per], axis=0).astype(BF16)
            return 0

        lax.fori_loop(0, PEER_HEADS, head, 0)

    nsub = eb // PEER_SUB
    a_per = PEER_SUB // nk

    def gate(j):
        cnts, grs = {}, {}
        for ai in range(a_per):
            a = e * (eb // nk) + j * a_per + ai
            for h in range(PEER_HEADS):
                cnts[ai, h] = jnp.broadcast_to(cnt_ref[h, pl.ds(a, 1), :], (BF16_ROWS, tm)).astype(BF16)
                grs[ai, h] = jnp.broadcast_to(g_ref[h, pl.ds(a, 1), :], (BF16_ROWS, tm)).astype(BF16)
        wgts = [[None] * ntile for _ in range(a_per)]
        for r in range(ntile):
            for h in range(PEER_HEADS):
                rk = rank_ref[h, r]
                ee = e2_ref[h, r]
                for ai in range(a_per):
                    term = jnp.where(rk < cnts[ai, h], ee * grs[ai, h], 0.0)
                    wgts[ai][r] = term if h == 0 else wgts[ai][r] + term
        return wgts

    def score(j):
        return lax.dot_general(u_ref[j * PEER_SUB:(j + 1) * PEER_SUB, :], xn_ref[...],
                               (((1,), (1,)), ((), ())), preferred_element_type=F32)

    def hidden(act, wgts):
        parts = []
        for ai in range(a_per):
            gel = _gelu_gate(act[ai * nk:(ai + 1) * nk, :]).astype(BF16)
            for r in range(ntile):
                parts.append(gel[r * BF16_ROWS:(r + 1) * BF16_ROWS, :] * wgts[ai][r])
        return jnp.concatenate(parts, axis=0)

    hid = hidden(score(0), gate(0))
    for j in range(nsub):
        if j + 1 < nsub:
            wgts = gate(j + 1)
            act = score(j + 1)
        acc_ref[...] += jnp.dot(vt_ref[j], hid, preferred_element_type=F32)
        if j + 1 < nsub:
            hid = hidden(act, wgts)

    @pl.when(e == ne - 1)
    def _fin():
        out = x_ref[...] + acc_ref[...].T
        if final_norm:
            out = _rms(out, fw_ref[...])
        o_ref[...] = out


def _peer_values(v):
    n_exp, d = v.shape
    return v.astype(BF16).reshape(n_exp // PEER_SUB, PEER_SUB, d).transpose(0, 2, 1)


def _peer(x2, norm_w, w_query, sub_keys, u_bf16, vt_bf16, final_w, final_norm, tm=512, eb=2048):
    t, d = x2.shape
    n_exp = u_bf16.shape[0]
    tm = min(tm, t)
    wqt = w_query.T.astype(BF16)
    keys = sub_keys.reshape(PEER_HEADS * 2, PEER_NKEYS, PEER_HALF).astype(F32)
    c2 = lambda i, e: (0, 0)
    head_buf = pltpu.VMEM((PEER_HEADS, PEER_NKEYS, tm), F32)
    packed_buf = pltpu.VMEM((PEER_HEADS, PEER_NKEYS // BF16_ROWS, BF16_ROWS, tm), BF16)
    return pl.pallas_call(
        functools.partial(_peer_kernel, final_norm=final_norm),
        grid=(t // tm, n_exp // eb),
        in_specs=[
            pl.BlockSpec((tm, d), lambda i, e: (i, 0)),
            pl.BlockSpec((1, d), c2),
            pl.BlockSpec(wqt.shape, c2),
            pl.BlockSpec(keys.shape, lambda i, e: (0, 0, 0)),
            pl.BlockSpec((eb, d), lambda i, e: (e, 0)),
            pl.BlockSpec((eb // PEER_SUB, d, PEER_SUB), lambda i, e: (e, 0, 0)),
            pl.BlockSpec((1, d), c2),
        ],
        out_specs=pl.BlockSpec((tm, d), lambda i, e: (i, 0)),
        out_shape=jax.ShapeDtypeStruct((t, d), F32),
        scratch_shapes=[pltpu.VMEM((tm, d), BF16), pltpu.VMEM((wqt.shape[0], tm), F32),
                        packed_buf, packed_buf, head_buf, head_buf, pltpu.VMEM((d, tm), F32)],
        compiler_params=_params("parallel", "arbitrary", flags=PEER_FLAGS),
        name="peer_dense",
    )(x2, norm_w.reshape(1, d), wqt, keys, u_bf16, vt_bf16, final_w.reshape(1, d))


def kernel(x, positions, norm_mix, w_in, s5_lam_re, s5_lam_im, s5_log_dt, s5_b_re, s5_b_im, s5_c_re, s5_c_im, s5_d, s5_w_glu, s5_b_glu, ret_gn_w, lru_conv_w, lru_conv_b, lru_w_r, lru_b_r, lru_w_i, lru_b_i, lru_lam, w_branch, w_out, norm_ffn, peer_w_query, peer_sub_keys, peer_u, peer_v, final_norm):
    bsz, seq, d = x.shape
    depth = norm_mix.shape[0]
    s5_w = s5_d.shape[1]
    v_w = ret_gn_w.shape[1]
    qk_w = v_w // 2
    lru_w = lru_lam.shape[1]
    q_col = s5_w
    x_col = q_col + 2 * qk_w + 2 * v_w
    gate_col = x_col + 2 * lru_w

    x2 = x.reshape(bsz * seq, d)
    cos_t, sin_t = _rope_tables(positions, qk_w // RET_HEADS // 2)
    for l in range(depth):
        proj = _norm_proj(x2, norm_mix[l], w_in[l].astype(BF16))
        s5c = _s5_consts(s5_lam_re[l], s5_lam_im[l], s5_log_dt[l], s5_b_re[l], s5_b_im[l],
                         s5_c_re[l], s5_c_im[l], min(S5_CHUNK, seq) // S5_SEGS)
        y_s5 = _s5_mixer(proj, bsz, seq, s5c, s5_d[l], s5_w_glu[l], s5_b_glu[l])
        y_ret = _ret_mixer(proj, bsz, seq, cos_t, sin_t, ret_gn_w[l], qk_w, v_w, q_col)
        y_lru = _lru_mixer(proj, bsz, seq, lru_conv_w[l], lru_conv_b[l], lru_w_r[l], lru_b_r[l],
                           lru_w_i[l], lru_b_i[l], lru_lam[l], lru_w, x_col)
        x2 = _merge((y_s5, y_ret, y_lru), proj, x2, w_branch[l], w_out[l], gate_col)
        x2 = _peer(x2, norm_ffn[l], peer_w_query[l], peer_sub_keys[l], peer_u[l].astype(BF16),
                   _peer_values(peer_v[l]), final_norm, final_norm=(l == depth - 1))
    return x2.reshape(bsz, seq, d)
```
